```python
import math
import jax, jax.numpy as jnp
from jax import lax
import numpy as np

D_MODEL = 1024
BATCH = 4
SEQ = 4096
DEPTH = 1
DEC_BATCH = 128
DEC_SEQ = 1
PAST_LEN = 2048
PAGE_SIZE = 128

D_MIX = D_MODEL
H_A = 4
DH_A = 64
D_A = H_A * 2 * DH_A
ROT_DIM = DH_A // 4
ROPE_THETA = 500000.0
H_B = 8
DH_B = 64
D_B = H_B * DH_B
LORA_W = 64
LORA_A = 64
LORA_G = 128
SHIFT_DIM = 3 * D_B + LORA_W + LORA_A + LORA_G
D_IN = 3 * D_A + SHIFT_DIM
D_FF = 2816
Q_BLOCK = 128
NORM_EPS = 1e-6
GN_EPS = 64e-5
B_SPLITS = [D_B, 2 * D_B, 3 * D_B, 3 * D_B + LORA_W, 3 * D_B + LORA_W + LORA_A]

kernel_name = 'hymba_diffattn_rwkv7_macaron_step'


def rmsnorm(x, g):
    xf = x.astype(jnp.float32)
    y = xf * lax.rsqrt(jnp.mean(xf * xf, axis=-1, keepdims=True) + NORM_EPS)
    return (y * g.astype(jnp.float32)).astype(x.dtype)


def swiglu(x, w_gate, w_up, w_down):
    return (jax.nn.silu(x @ w_gate) * (x @ w_up)) @ w_down


def partial_rope(x, pos):
    half = ROT_DIM // 2
    inv_freq = ROPE_THETA ** (-jnp.arange(half, dtype=jnp.float32) / half)
    ang = pos.astype(jnp.float32)[:, None] * inv_freq[None, :]
    cos = jnp.cos(ang)[:, None, None, :].astype(x.dtype)
    sin = jnp.sin(ang)[:, None, None, :].astype(x.dtype)
    x1, x2, rest = x[..., :half], x[..., half:ROT_DIM], x[..., ROT_DIM:]
    return jnp.concatenate([x1 * cos - x2 * sin, x2 * cos + x1 * sin, rest], axis=-1)


def diff_softmax_attend(q, k, v, q_pos, k_pos, lam):
    s = jnp.einsum('nqhjd,nkhjd->nhjqk', q, k, preferred_element_type=jnp.float32) * (DH_A ** -0.5)
    mask = k_pos[None, :] <= q_pos[:, None]
    p = jax.nn.softmax(jnp.where(mask, s, -jnp.inf), axis=-1)
    p = p[:, :, 0] - lam * p[:, :, 1]
    return jnp.einsum('nhqk,nkhe->nqhe', p.astype(v.dtype), v)


def attend(q, k, v, q_pos, k_pos, lam):
    n, t = q.shape[0], q.shape[1]
    if t > Q_BLOCK and t % Q_BLOCK == 0:
        nb = t // Q_BLOCK
        qb = jnp.moveaxis(q.reshape(n, nb, Q_BLOCK, H_A, 2, DH_A), 1, 0)
        pb = q_pos.reshape(nb, Q_BLOCK)
        out = lax.map(lambda blk: diff_softmax_attend(blk[0], k, v, blk[1], k_pos, lam), (qb, pb))
        return jnp.moveaxis(out, 0, 1).reshape(n, t, H_A, 2 * DH_A)
    return diff_softmax_attend(q, k, v, q_pos, k_pos, lam)


def wkv_scan(s0, r, decay, k, v, kk, a):
    def step(s, inp):
        r_t, w_t, k_t, v_t, kk_t, a_t = inp
        sa = jnp.einsum('nhvk,nhk->nhv', s, -kk_t)
        s = (s * w_t[:, :, None, :] + sa[..., None] * (kk_t * a_t)[:, :, None, :]
             + v_t[..., None] * k_t[:, :, None, :])
        return s, jnp.einsum('nhvk,nhk->nhv', s, r_t)
    xs = tuple(jnp.moveaxis(z, 1, 0) for z in (r, decay, k, v, kk, a))
    s, ys = lax.scan(step, s0, xs)
    return s, jnp.moveaxis(ys, 0, 1)


def rwkv7_time_mix(pb, prev, s0, lp):
    n, t, _ = pb.shape
    f32 = jnp.float32
    shifted = jnp.concatenate([prev[:, None, :].astype(pb.dtype), pb[:, :-1]], axis=1)
    xs = pb + (shifted - pb) * lp['mu_shift']
    r, k, v, xw, xa, xg = jnp.split(xs, B_SPLITS, axis=-1)
    w = -jax.nn.softplus(-(lp['w0'] + jnp.tanh(xw) @ lp['w2']).astype(f32)) - 0.5
    decay = jnp.exp(-jnp.exp(w))
    a = jax.nn.sigmoid((lp['a0'] + xa @ lp['a2']).astype(f32))
    g = jax.nn.sigmoid(xg) @ lp['g2']
    heads = lambda z: z.reshape(n, t, H_B, DH_B).astype(f32)
    r, k, v, decay, a = heads(r), heads(k), heads(v), heads(decay), heads(a)
    kk = k * lp['k_k'].astype(f32).reshape(H_B, DH_B)
    kk = kk / jnp.maximum(jnp.sqrt(jnp.sum(kk * kk, axis=-1, keepdims=True)), 1e-12)
    k = k * (1.0 + (a - 1.0) * lp['k_a'].astype(f32).reshape(H_B, DH_B))
    s, y = wkv_scan(s0.astype(f32), r, decay, k, v, kk, a)
    mean = jnp.mean(y, axis=-1, keepdims=True)
    var = jnp.mean(jnp.square(y - mean), axis=-1, keepdims=True)
    y = ((y - mean) * lax.rsqrt(var + GN_EPS) * lp['ln_x_w'].astype(f32).reshape(H_B, DH_B)
         + lp['ln_x_b'].astype(f32).reshape(H_B, DH_B))
    y = y + jnp.sum(r * k * lp['r_k'].astype(f32), axis=-1, keepdims=True) * v
    y = y.reshape(n, t, D_B).astype(pb.dtype) * g
    return y, s.astype(s0.dtype), pb[:, -1]


def decoder_layer(x, pos, past_k, past_v, past_pos, prev_shift, s0, lp, lam_init):
    n, t, _ = x.shape
    f32 = jnp.float32
    h = x + 0.5 * rmsnorm(swiglu(rmsnorm(x, lp['n_ffn1_pre']), lp['ffn1_gate'], lp['ffn1_up'],
                                 lp['ffn1_down']), lp['n_ffn1_post'])
    u = rmsnorm(h, lp['n_mix_pre'])
    proj = u @ lp['w_in']
    q, k, v = jnp.split(proj[..., :3 * D_A], 3, axis=-1)
    pb = proj[..., 3 * D_A:]
    q = partial_rope(q.reshape(n, t, H_A, 2, DH_A), pos)
    k = partial_rope(k.reshape(n, t, H_A, 2, DH_A), pos)
    v = v.reshape(n, t, H_A, 2 * DH_A)
    if past_k is None:
        k_all, v_all, k_pos = k, v, pos
    else:
        k_all = jnp.concatenate([past_k.astype(k.dtype), k], axis=1)
        v_all = jnp.concatenate([past_v.astype(v.dtype), v], axis=1)
        k_pos = jnp.concatenate([past_pos, pos])
    lam = (jnp.exp(jnp.sum(lp['lambda_q1'].astype(f32) * lp['lambda_k1'].astype(f32)))
           - jnp.exp(jnp.sum(lp['lambda_q2'].astype(f32) * lp['lambda_k2'].astype(f32))) + lam_init)
    ya = attend(q, k_all, v_all, pos, k_pos, lam)
    ya = (rmsnorm(ya, lp['subln']) * (1.0 - lam_init)).reshape(n, t, D_A)
    yb, s_new, shift_new = rwkv7_time_mix(pb, prev_shift, s0, lp)
    mix = jnp.concatenate([ya, yb], axis=-1) @ lp['w_out']
    h = h + rmsnorm(mix, lp['n_mix_post'])
    h = h + 0.5 * rmsnorm(swiglu(rmsnorm(h, lp['n_ffn2_pre']), lp['ffn2_gate'], lp['ffn2_up'],
                                 lp['ffn2_down']), lp['n_ffn2_post'])
    return h, k.reshape(n, t, H_A, 2 * DH_A), v, s_new, shift_new


def setup_inputs(seed: int = 0) -> dict:
    key = jax.random.key(seed)
    ks = iter(jax.random.split(key, 64))
    f32 = jnp.float32
    nrm = lambda shape, scale: scale * jax.random.normal(next(ks), shape, f32)
    gain = lambda shape: 1.0 + nrm(shape, 0.05)
    n_pages = PAST_LEN // PAGE_SIZE
    n_used = DEC_BATCH * n_pages
    n_pool = n_used + (n_used + 3) // 4
    x_prompt = nrm((BATCH, SEQ, D_MODEL), 1.0)
    x_sample = nrm((DEC_BATCH, DEC_SEQ, D_MODEL), 1.0)
    cache_k = nrm((DEPTH, n_pool, PAGE_SIZE, H_A, 2 * DH_A), 1.0)
    cache_v = nrm((DEPTH, n_pool, PAGE_SIZE, H_A, 2 * DH_A), 1.0)
    state_wkv = nrm((DEPTH, DEC_BATCH, H_B, DH_B, DH_B), 0.3)
    state_shift = nrm((DEPTH, DEC_BATCH, SHIFT_DIM), 1.0)
    page_table = jax.random.permutation(next(ks), n_pool)[:n_used].reshape(DEC_BATCH, n_pages).astype(jnp.int32)
    return {
        'x_prompt': x_prompt, 'x_sample': x_sample,
        'cache_k': cache_k, 'cache_v': cache_v,
        'state_wkv': state_wkv, 'state_shift': state_shift,
        'page_table': page_table,
        'n_ffn1_pre': gain((DEPTH, D_MODEL)), 'n_ffn1_post': gain((DEPTH, D_MODEL)),
        'ffn1_gate': nrm((DEPTH, D_MODEL, D_FF), D_MODEL ** -0.5),
        'ffn1_up': nrm((DEPTH, D_MODEL, D_FF), D_MODEL ** -0.5),
        'ffn1_down': nrm((DEPTH, D_FF, D_MODEL), D_FF ** -0.5),
        'n_mix_pre': gain((DEPTH, D_MODEL)), 'n_mix_post': gain((DEPTH, D_MODEL)),
        'w_in': nrm((DEPTH, D_MODEL, D_IN), D_MODEL ** -0.5),
        'w_out': nrm((DEPTH, D_MIX, D_MODEL), D_MIX ** -0.5),
        'lambda_q1': nrm((DEPTH, DH_A), 0.1), 'lambda_k1': nrm((DEPTH, DH_A), 0.1),
        'lambda_q2': nrm((DEPTH, DH_A), 0.1), 'lambda_k2': nrm((DEPTH, DH_A), 0.1),
        'subln': gain((DEPTH, 2 * DH_A)),
        'mu_shift': jax.random.uniform(next(ks), (DEPTH, SHIFT_DIM), f32),
        'w0': nrm((DEPTH, D_B), 0.5),
        'w2': nrm((DEPTH, LORA_W, D_B), 0.1),
        'a0': nrm((DEPTH, D_B), 0.5),
        'a2': nrm((DEPTH, LORA_A, D_B), 0.5 * LORA_A ** -0.5),
        'g2': nrm((DEPTH, LORA_G, D_B), LORA_G ** -0.5),
        'k_k': 0.85 + nrm((DEPTH, D_B), 0.05),
        'k_a': 1.0 + nrm((DEPTH, D_B), 0.05),
        'r_k': nrm((DEPTH, H_B, DH_B), 0.1),
        'ln_x_w': gain((DEPTH, D_B)), 'ln_x_b': nrm((DEPTH, D_B), 0.02),
        'n_ffn2_pre': gain((DEPTH, D_MODEL)), 'n_ffn2_post': gain((DEPTH, D_MODEL)),
        'ffn2_gate': nrm((DEPTH, D_MODEL, D_FF), D_MODEL ** -0.5),
        'ffn2_up': nrm((DEPTH, D_MODEL, D_FF), D_MODEL ** -0.5),
        'ffn2_down': nrm((DEPTH, D_FF, D_MODEL), D_FF ** -0.5),
    }


def reference(x_prompt, x_sample, cache_k, cache_v, state_wkv, state_shift, page_table,
              n_ffn1_pre, n_ffn1_post, ffn1_gate, ffn1_up, ffn1_down,
              n_mix_pre, n_mix_post, w_in, w_out,
              lambda_q1, lambda_k1, lambda_q2, lambda_k2, subln,
              mu_shift, w0, w2, a0, a2, g2, k_k, k_a, r_k, ln_x_w, ln_x_b,
              n_ffn2_pre, n_ffn2_post, ffn2_gate, ffn2_up, ffn2_down):
    n_p, t_p = x_prompt.shape[0], x_prompt.shape[1]
    n_s, t_s = x_sample.shape[0], x_sample.shape[1]
    n_pages = page_table.shape[1]
    past_len = n_pages * PAGE_SIZE
    pos_p = jnp.arange(t_p, dtype=jnp.int32)
    pos_s = past_len + jnp.arange(t_s, dtype=jnp.int32)
    pos_past = jnp.arange(past_len, dtype=jnp.int32)
    yp, ys = x_prompt, x_sample
    kp_l, vp_l, sp_l, hp_l, ks_l, vs_l, ss_l, hs_l = [], [], [], [], [], [], [], []
    for l in range(DEPTH):
        lp = {
            'n_ffn1_pre': n_ffn1_pre[l], 'n_ffn1_post': n_ffn1_post[l],
            'ffn1_gate': ffn1_gate[l], 'ffn1_up': ffn1_up[l], 'ffn1_down': ffn1_down[l],
            'n_mix_pre': n_mix_pre[l], 'n_mix_post': n_mix_post[l],
            'w_in': w_in[l], 'w_out': w_out[l],
            'lambda_q1': lambda_q1[l], 'lambda_k1': lambda_k1[l],
            'lambda_q2': lambda_q2[l], 'lambda_k2': lambda_k2[l], 'subln': subln[l],
            'mu_shift': mu_shift[l], 'w0': w0[l], 'w2': w2[l], 'a0': a0[l], 'a2': a2[l],
            'g2': g2[l], 'k_k': k_k[l], 'k_a': k_a[l], 'r_k': r_k[l],
            'ln_x_w': ln_x_w[l], 'ln_x_b': ln_x_b[l],
            'n_ffn2_pre': n_ffn2_pre[l], 'n_ffn2_post': n_ffn2_post[l],
            'ffn2_gate': ffn2_gate[l], 'ffn2_up': ffn2_up[l], 'ffn2_down': ffn2_down[l],
        }
        lam_init = 0.8 - 0.6 * math.exp(-0.3 * l)
        prev0 = jnp.zeros((n_p, SHIFT_DIM), x_prompt.dtype)
        s0 = jnp.zeros((n_p, H_B, DH_B, DH_B), state_wkv.dtype)
        yp, kp, vp, sp, hp = decoder_layer(yp, pos_p, None, None, None, prev0, s0, lp, lam_init)
        past_k = cache_k[l][page_table].reshape(n_s, past_len, H_A, 2, DH_A)
        past_v = cache_v[l][page_table].reshape(n_s, past_len, H_A, 2 * DH_A)
        ys, kq, vq, sq, hq = decoder_layer(ys, pos_s, past_k, past_v, pos_past,
                                           state_shift[l], state_wkv[l], lp, lam_init)
        kp_l.append(kp); vp_l.append(vp); sp_l.append(sp); hp_l.append(hp)
        ks_l.append(kq); vs_l.append(vq); ss_l.append(sq); hs_l.append(hq)
    return (yp, ys,
            jnp.stack(kp_l), jnp.stack(vp_l), jnp.stack(sp_l), jnp.stack(hp_l),
            jnp.stack(ks_l), jnp.stack(vs_l), jnp.stack(ss_l), jnp.stack(hs_l))
```

```python
import functools
import math

import jax
import jax.numpy as jnp
from jax import lax
from jax.experimental import pallas as pl
from jax.experimental.pallas import tpu as pltpu

F32 = jnp.float32
BF16 = jnp.bfloat16

D_MODEL = 1024
H_A = 4
DH_A = 64
D_A = H_A * 2 * DH_A
ROT_DIM = DH_A // 4
ROPE_THETA = 500000.0
H_B = 8
DH_B = 64
D_B = H_B * DH_B
LORA_W = 64
LORA_A = 64
LORA_G = 128
SHIFT_DIM = 3 * D_B + LORA_W + LORA_A + LORA_G
D_IN = 3 * D_A + SHIFT_DIM
D_FF = 2816
PAGE_SIZE = 128
NORM_EPS = 1e-6
GN_EPS = 64e-5

LANES = 128
SUBLANES = 8
VMEM_LIMIT = 48 * 1024 * 1024


def _dot(a, b):
    return jnp.dot(a, b, preferred_element_type=F32)


def _dot_nt(a, b):
    return lax.dot_general(a, b, (((1,), (1,)), ((), ())), preferred_element_type=F32)


def _rms(x, g):
    return x * lax.rsqrt(jnp.mean(x * x, axis=-1, keepdims=True) + NORM_EPS) * g


def _sigmoid(x):
    return 1.0 / (1.0 + jnp.exp(-x))


def _split2(x):
    hi = x.astype(BF16)
    mid = (x - hi.astype(F32)).astype(BF16)
    return hi, mid


def _segsum(x, ones):
    hi = x.astype(BF16)
    r1 = x - hi.astype(F32)
    mid = r1.astype(BF16)
    lo = (r1 - mid.astype(F32)).astype(BF16)
    return _dot(hi, ones) + _dot(mid, ones) + _dot(lo, ones)


def _params(sem):
    return pltpu.CompilerParams(dimension_semantics=sem, vmem_limit_bytes=VMEM_LIMIT)


def _ffn_kernel(x_ref, gpre_ref, gpost_ref, wg_ref, wu_ref, wd_ref, o_ref, un_ref, acc_ref, *, nf):
    k = pl.program_id(1)

    @pl.when(k == 0)
    def _():
        un_ref[...] = _rms(x_ref[...], gpre_ref[...]).astype(BF16)
        acc_ref[...] = jnp.zeros_like(acc_ref)

    un = un_ref[...]
    g = _dot(un, wg_ref[...])
    u = _dot(un, wu_ref[...])
    hid = (g * _sigmoid(g)) * u
    acc_ref[...] += _dot(hid.astype(BF16), wd_ref[...])

    @pl.when(k == nf - 1)
    def _():
        o_ref[...] = x_ref[...] + 0.5 * _rms(acc_ref[...], gpost_ref[...])


def _ffn(x, g_pre, g_post, wg, wu, wd, tm):
    m = x.shape[0]
    tf = 256
    nf = D_FF // tf
    row = pl.BlockSpec((tm, D_MODEL), lambda i, k: (i, 0))
    vec = pl.BlockSpec((1, D_MODEL), lambda i, k: (0, 0))
    return pl.pallas_call(
        functools.partial(_ffn_kernel, nf=nf),
        grid=(m // tm, nf),
        in_specs=[row, vec, vec,
                  pl.BlockSpec((D_MODEL, tf), lambda i, k: (0, k)),
                  pl.BlockSpec((D_MODEL, tf), lambda i, k: (0, k)),
                  pl.BlockSpec((tf, D_MODEL), lambda i, k: (k, 0))],
        out_specs=row,
        out_shape=jax.ShapeDtypeStruct((m, D_MODEL), F32),
        scratch_shapes=[pltpu.VMEM((tm, D_MODEL), BF16), pltpu.VMEM((tm, D_MODEL), F32)],
        compiler_params=_params(("parallel", "arbitrary")),
        name="ffn",
    )(x, g_pre, g_post, wg, wu, wd)


def _proj_kernel(h_ref, g_ref, w_ref, cos_ref, sina_ref, sinb_ref,
                 q_ref, k_ref, v_ref, kb_ref, vb_ref, pb_ref):
    u = _rms(h_ref[...], g_ref[...]).astype(BF16)
    cos = cos_ref[...]
    sina = sina_ref[...]
    sinb = sinb_ref[...]
    half = ROT_DIM // 2

    def rope(x):
        return x * cos + pltpu.roll(x, LANES - half, 1) * sina + pltpu.roll(x, half, 1) * sinb

    qa = _dot(u, w_ref[:, 0:D_A])
    ka = _dot(u, w_ref[:, D_A:2 * D_A])
    for hh in range(H_A):
        sl = slice(hh * LANES, (hh + 1) * LANES)
        q_ref[:, sl] = (rope(qa[:, sl]) * (DH_A ** -0.5)).astype(BF16)
        kh = rope(ka[:, sl])
        k_ref[:, sl] = kh
        kb_ref[:, sl] = kh.astype(BF16)
    va = _dot(u, w_ref[:, 2 * D_A:3 * D_A])
    v_ref[...] = va
    vb_ref[...] = va.astype(BF16)
    pb_ref[...] = _dot(u, w_ref[:, 3 * D_A:])


def _proj(h, g, w_in, tables, tm, table_blocks):
    m = h.shape[0]
    row = lambda width: pl.BlockSpec((tm, width), lambda i: (i, 0))
    tab = pl.BlockSpec((tm, LANES), lambda i: (i % table_blocks, 0))
    shp = lambda width, dt: jax.ShapeDtypeStruct((m, width), dt)
    return pl.pallas_call(
        _proj_kernel,
        grid=(m // tm,),
        in_specs=[row(D_MODEL), pl.BlockSpec((1, D_MODEL), lambda i: (0, 0)),
                  pl.BlockSpec((D_MODEL, D_IN), lambda i: (0, 0)), tab, tab, tab],
        out_specs=[row(D_A), row(D_A), row(D_A), row(D_A), row(D_A), row(SHIFT_DIM)],
        out_shape=[shp(D_A, BF16), shp(D_A, F32), shp(D_A, F32), shp(D_A, BF16), shp(D_A, BF16),
                   shp(SHIFT_DIM, F32)],
        compiler_params=_params(("parallel",)),
        name="proj",
    )(h, g, w_in, *tables)


def _rope_tables(pos):
    half = ROT_DIM // 2
    t = pos.shape[0]
    inv_freq = ROPE_THETA ** (-jnp.arange(half, dtype=F32) / half)
    ang = pos.astype(F32)[:, None] * inv_freq[None, :]
    cos = jnp.cos(ang)
    sin = jnp.sin(ang)
    rest = DH_A - ROT_DIM
    cos64 = jnp.concatenate([cos, cos, jnp.ones((t, rest), F32)], axis=1)
    sina64 = jnp.concatenate([-sin, jnp.zeros((t, half + rest), F32)], axis=1)
    sinb64 = jnp.concatenate([jnp.zeros((t, half), F32), sin, jnp.zeros((t, rest), F32)], axis=1)
    two = lambda x: jnp.concatenate([x, x], axis=1)
    return two(cos64), two(sina64), two(sinb64)


def _lambda(lq1_ref, lk1_ref, lq2_ref, lk2_ref, lam_init):
    s1 = jnp.sum(lq1_ref[...] * lk1_ref[...], axis=-1, keepdims=True)
    s2 = jnp.sum(lq2_ref[...] * lk2_ref[...], axis=-1, keepdims=True)
    return jnp.exp(s1) - jnp.exp(s2) + lam_init


def _attn_kernel(q_ref, k_ref, v_ref, lq1_ref, lk1_ref, lq2_ref, lk2_ref, subln_ref, o_ref,
                 m_ref, l_ref, acc_ref, *, tq, lam_init):
    qi = pl.program_id(2)
    q = q_ref[...].astype(F32)
    lane = lax.broadcasted_iota(jnp.int32, q.shape, 1)
    qs = (jnp.where(lane < DH_A, q, 0.0).astype(BF16), jnp.where(lane >= DH_A, q, 0.0).astype(BF16))
    m_ref[...] = jnp.full(m_ref.shape, -jnp.inf, F32)
    l_ref[...] = jnp.zeros_like(l_ref)
    acc_ref[...] = jnp.zeros_like(acc_ref)
    row = lax.broadcasted_iota(jnp.int32, (tq, tq), 0)
    col = lax.broadcasted_iota(jnp.int32, (tq, tq), 1)

    def block(kstart, diagonal):
        k = k_ref[pl.ds(kstart, tq), :]
        v = v_ref[pl.ds(kstart, tq), :]
        for j in range(2):
            s = _dot_nt(qs[j], k)
            if diagonal:
                s = jnp.where(col <= row, s, -jnp.inf)
            m_prev = m_ref[j]
            m_new = jnp.maximum(m_prev, jnp.max(s, axis=1, keepdims=True))
            alpha = jnp.exp(m_prev - m_new)
            p = jnp.exp(s - m_new)
            l_ref[j] = alpha * l_ref[j] + jnp.sum(p, axis=1, keepdims=True)
            acc_ref[j] = alpha * acc_ref[j] + _dot(p.astype(BF16), v)
            m_ref[j] = m_new

    def body(ki, carry):
        block(pl.multiple_of(ki * tq, tq), False)
        return carry

    lax.fori_loop(0, qi, body, 0)
    block(pl.multiple_of(qi * tq, tq), True)

    lam = _lambda(lq1_ref, lk1_ref, lq2_ref, lk2_ref, lam_init)
    o = acc_ref[0] / l_ref[0] - lam * (acc_ref[1] / l_ref[1])
    o_ref[...] = (_rms(o, subln_ref[...]) * (1.0 - lam_init)).astype(BF16)


def _attn_prompt(q, kb, vb, lams, subln, n, t, lam_init):
    tq = 512
    nq = t // tq
    qspec = pl.BlockSpec((tq, LANES), lambda b, h, i: (b * nq + i, h))
    kvspec = pl.BlockSpec((t, LANES), lambda b, h, i: (b, h))
    small = lambda w: pl.BlockSpec((1, w), lambda b, h, i: (0, 0))
    return pl.pallas_call(
        functools.partial(_attn_kernel, tq=tq, lam_init=lam_init),
        grid=(n, H_A, nq),
        in_specs=[qspec, kvspec, kvspec, small(DH_A), small(DH_A), small(DH_A), small(DH_A),
                  small(2 * DH_A)],
        out_specs=qspec,
        out_shape=jax.ShapeDtypeStruct((n * t, D_A), BF16),
        scratch_shapes=[pltpu.VMEM((2, tq, 1), F32), pltpu.VMEM((2, tq, 1), F32),
                        pltpu.VMEM((2, tq, LANES), F32)],
        compiler_params=_params(("parallel", "parallel", "arbitrary")),
        name="attn_prompt",
    )(q, kb, vb, *lams, subln)


def _attn_decode_kernel(pt_ref, q_ref, ks_ref, vs_ref, lq1_ref, lk1_ref, lq2_ref, lk2_ref, subln_ref,
                        *rest, n_pages, lam_init):
    del pt_ref
    kp_refs = rest[:n_pages]
    vp_refs = rest[n_pages:2 * n_pages]
    o_ref = rest[2 * n_pages]
    q = q_ref[0].astype(F32)
    ks = ks_ref[0].astype(BF16).astype(F32)
    vs = vs_ref[0].astype(BF16).astype(F32)
    lam = _lambda(lq1_ref, lk1_ref, lq2_ref, lk2_ref, lam_init)
    r2 = lax.broadcasted_iota(jnp.int32, (2, LANES), 0)
    l2 = lax.broadcasted_iota(jnp.int32, (2, LANES), 1)
    for hh in range(H_A):
        sl = slice(hh * LANES, (hh + 1) * LANES)
        qh = jnp.where((l2 // DH_A) == r2, jnp.broadcast_to(q[:, sl], (2, LANES)), 0.0)
        qh_b = qh.astype(BF16)
        s = jnp.concatenate([_dot_nt(qh_b, kp_refs[pg][:, hh, :].astype(BF16)) for pg in range(n_pages)],
                            axis=1)
        s_self = jnp.sum(qh * ks[:, sl], axis=1, keepdims=True)
        m = jnp.maximum(jnp.max(s, axis=1, keepdims=True), s_self)
        e = jnp.exp(s - m)
        e_self = jnp.exp(s_self - m)
        inv = 1.0 / (jnp.sum(e, axis=1, keepdims=True) + e_self)
        p = e * inv
        p_self = e_self * inv
        pc = (p[0:1] - lam * p[1:2]).astype(BF16)
        pc_self = (p_self[0:1] - lam * p_self[1:2]).astype(BF16).astype(F32)
        o = pc_self * vs[:, sl]
        for pg in range(n_pages):
            o = o + _dot(pc[:, pg * PAGE_SIZE:(pg + 1) * PAGE_SIZE], vp_refs[pg][:, hh, :].astype(BF16))
        o_ref[0, :, sl] = (_rms(o, subln_ref[...]) * (1.0 - lam_init)).astype(BF16)


def _attn_decode(q, k_self, v_self, cache_k, cache_v, layer, page_table, lams, subln, lam_init):
    nb, n_pages = page_table.shape
    tok = pl.BlockSpec((1, 1, D_A), lambda b, pt: (b, 0, 0))
    small = lambda w: pl.BlockSpec((1, w), lambda b, pt: (0, 0))
    page = lambda p: pl.BlockSpec((None, None, PAGE_SIZE, H_A, 2 * DH_A),
                                  lambda b, pt: (layer, pt[b, p], 0, 0, 0))
    pages = [page(p) for p in range(n_pages)]
    grid_spec = pltpu.PrefetchScalarGridSpec(
        num_scalar_prefetch=1,
        grid=(nb,),
        in_specs=[tok, tok, tok, small(DH_A), small(DH_A), small(DH_A), small(DH_A), small(2 * DH_A)]
        + pages + pages,
        out_specs=tok,
    )
    out = pl.pallas_call(
        functools.partial(_attn_decode_kernel, n_pages=n_pages, lam_init=lam_init),
        grid_spec=grid_spec,
        out_shape=jax.ShapeDtypeStruct((nb, 1, D_A), BF16),
        compiler_params=_params(("arbitrary",)),
        name="attn_decode",
    )(page_table, q.reshape(nb, 1, D_A), k_self.reshape(nb, 1, D_A), v_self.reshape(nb, 1, D_A),
      *lams, subln, *([cache_k] * n_pages), *([cache_v] * n_pages))
    return out.reshape(nb, D_A)


def _prep_kernel(*refs, tm, tiles_per_seq):
    if tiles_per_seq:
        pb_ref, prev_ref, tail_ref = refs[:3]
        refs = refs[3:]
    else:
        pb_ref, prev_ref = refs[:2]
        refs = refs[2:]
    (mu_ref, w0_ref, a0_ref, w2_ref, a2_ref, g2_ref, kk_w_ref, ka_w_ref, rk_w_ref, ones_ref,
     kk_o, w_o, b_o, k_o, v_o, c_o, kr_o, bv_o, g_o) = refs
    pb = pb_ref[...]
    if tiles_per_seq:
        first = (pl.program_id(0) % tiles_per_seq) == 0
        prev_row = jnp.where(first, prev_ref[0], tail_ref[SUBLANES - 1:SUBLANES, :])
        rows = lax.broadcasted_iota(jnp.int32, pb.shape, 0)
        shifted = jnp.where(rows == 0, jnp.broadcast_to(prev_row, pb.shape), pltpu.roll(pb, 1, 0))
    else:
        shifted = prev_ref[...]
    xs = pb + (shifted - pb) * mu_ref[...]
    r = xs[:, 0:D_B]
    k = xs[:, D_B:2 * D_B]
    v = xs[:, 2 * D_B:3 * D_B]
    xwa = xs[:, 3 * D_B:3 * D_B + LORA_W + LORA_A]
    xg = xs[:, 3 * D_B + LORA_W + LORA_A:]
    ones = ones_ref[...]
    w_raw = w0_ref[...] + _dot(jnp.tanh(xwa).astype(BF16), w2_ref[...])
    z = -w_raw
    softplus = jnp.maximum(z, 0.0) + jnp.log(1.0 + jnp.exp(-jnp.abs(z)))
    decay = jnp.exp(-jnp.exp(-softplus - 0.5))
    a = _sigmoid(a0_ref[...] + _dot(xwa.astype(BF16), a2_ref[...]))
    g = _dot(_sigmoid(xg).astype(BF16), g2_ref[...])
    kk = k * kk_w_ref[...]
    kk = kk / jnp.maximum(jnp.sqrt(_segsum(kk * kk, ones)), 1e-12)
    k2 = k * (1.0 + (a - 1.0) * ka_w_ref[...])
    b = kk * a
    br = _segsum(b * r, ones)
    kr = _segsum(k2 * r, ones)
    bonus = _segsum(r * k2 * rk_w_ref[...], ones)
    kk_o[...] = kk
    w_o[...] = decay
    b_o[...] = b
    k_o[...] = k2
    v_o[...] = v
    c_o[...] = decay * r - kk * br
    kr_o[...] = kr
    bv_o[...] = bonus * v
    g_o[...] = g


def _rwkv_prep(pb, prev, weights, tm, seq_len):
    m = pb.shape[0]
    row = lambda w: pl.BlockSpec((tm, w), lambda i: (i, 0))
    const = lambda a: pl.BlockSpec(a.shape, lambda i: (0,) * a.ndim)
    if seq_len > 1:
        tiles_per_seq = seq_len // tm
        tail = pl.BlockSpec((SUBLANES, SHIFT_DIM),
                            lambda i: (jnp.maximum(i * (tm // SUBLANES) - 1, 0), 0))
        head = [row(SHIFT_DIM), pl.BlockSpec((1, 1, SHIFT_DIM), lambda i: (i // tiles_per_seq, 0, 0)), tail]
        args = [pb, prev, pb]
    else:
        tiles_per_seq = 0
        head = [row(SHIFT_DIM), row(SHIFT_DIM)]
        args = [pb, prev]
    return pl.pallas_call(
        functools.partial(_prep_kernel, tm=tm, tiles_per_seq=tiles_per_seq),
        grid=(m // tm,),
        in_specs=head + [const(a) for a in weights],
        out_specs=[row(D_B)] * 9,
        out_shape=[jax.ShapeDtypeStruct((m, D_B), F32)] * 9,
        compiler_params=_params(("parallel",)),
        name="rwkv_prep",
    )(*args, *weights)


PAIRS = H_B // 2
STEP_UNROLL = SUBLANES


def _scan_kernel(kk_ref, w_ref, b_ref, k_ref, v_ref, c_ref, kr_ref, ones_ref, y_ref, sout_ref, s_ref,
                 *, nb, tb):
    step_blk = pl.program_id(0)

    @pl.when(step_blk == 0)
    def _():
        s_ref[...] = jnp.zeros_like(s_ref)

    ones = ones_ref[...]
    sub = lax.broadcasted_iota(jnp.int32, (DH_B, LANES), 0)
    lan = lax.broadcasted_iota(jnp.int32, (DH_B, LANES), 1)
    diag = (sub == (lan % DH_B)).astype(F32)

    def trip(g, carry):
        t0 = pl.multiple_of(g * STEP_UNROLL, STEP_UNROLL)
        tiles = {}
        for n in range(nb):
            for p in range(PAIRS):
                sl = slice(p * LANES, (p + 1) * LANES)
                tiles[n, p] = tuple(ref[n, pl.ds(t0, STEP_UNROLL), sl]
                                    for ref in (kk_ref, w_ref, b_ref, k_ref, v_ref, c_ref, kr_ref))
        y_rows = {key: [] for key in tiles}
        for j in range(STEP_UNROLL):
            lhs = []
            for n in range(nb):
                for p in range(PAIRS):
                    kk, _, _, _, v, c, _ = (x[j:j + 1, :] for x in tiles[n, p])
                    s = s_ref[n * PAIRS + p]
                    for part in (s * kk, s * c, diag * v):
                        hi, mid = _split2(part)
                        lhs.append(jnp.concatenate([hi, mid], axis=1))
            red = _dot(jnp.concatenate(lhs, axis=0), ones)
            for n in range(nb):
                for p in range(PAIRS):
                    idx = n * PAIRS + p
                    _, w, b, k, v, _, kr = (x[j:j + 1, :] for x in tiles[n, p])
                    base = idx * 3 * DH_B
                    sa = -red[base:base + DH_B]
                    z = red[base + DH_B:base + 2 * DH_B]
                    vb = red[base + 2 * DH_B:base + 3 * DH_B]
                    s_ref[idx] = s_ref[idx] * w + sa * b + vb * k
                    y_rows[n, p].append(jnp.sum(diag * z, axis=0, keepdims=True) + v * kr)
        for n in range(nb):
            for p in range(PAIRS):
                y_ref[n, pl.ds(t0, STEP_UNROLL), p * LANES:(p + 1) * LANES] = (
                    jnp.concatenate(y_rows[n, p], axis=0))
        return carry

    lax.fori_loop(0, tb // STEP_UNROLL, trip, 0)

    @pl.when(step_blk == pl.num_programs(0) - 1)
    def _():
        for n in range(nb):
            for p in range(PAIRS):
                s = s_ref[n * PAIRS + p]
                sout_ref[n, 2 * p] = s[:, :DH_B]
                sout_ref[n, 2 * p + 1] = s[:, DH_B:]


def _rwkv_scan(vecs, ones2, n, t):
    tb = 128
    blk = pl.BlockSpec((n, tb, D_B), lambda i: (0, i, 0))
    return pl.pallas_call(
        functools.partial(_scan_kernel, nb=n, tb=tb),
        grid=(t // tb,),
        in_specs=[blk] * 7 + [pl.BlockSpec(ones2.shape, lambda i: (0, 0))],
        out_specs=[blk, pl.BlockSpec((n, H_B, DH_B, DH_B), lambda i: (0, 0, 0, 0))],
        out_shape=[jax.ShapeDtypeStruct((n, t, D_B), F32),
                   jax.ShapeDtypeStruct((n, H_B, DH_B, DH_B), F32)],
        scratch_shapes=[pltpu.VMEM((n * PAIRS, DH_B, LANES), F32)],
        compiler_params=_params(("arbitrary",)),
        name="rwkv_scan",
    )(*[x.reshape(n, t, D_B) for x in vecs], ones2)


def _wkv_step_kernel(s_ref, kk_ref, w_ref, b_ref, k_ref, v_ref, c_ref, kr_ref, y_ref, so_ref, *, nh):
    sub = lax.broadcasted_iota(jnp.int32, (DH_B, DH_B), 0)
    lan = lax.broadcasted_iota(jnp.int32, (DH_B, DH_B), 1)
    diag = (sub == lan).astype(F32)

    def body(i, carry):
        s = s_ref[i]
        v = v_ref[i]
        sa = -jnp.sum(s * kk_ref[i], axis=1, keepdims=True)
        z = jnp.sum(s * c_ref[i], axis=1, keepdims=True)
        vcol = jnp.sum(diag * v, axis=1, keepdims=True)
        so_ref[i] = s * w_ref[i] + sa * b_ref[i] + vcol * k_ref[i]
        y_ref[i] = jnp.sum(diag * z, axis=0, keepdims=True) + v * kr_ref[i]
        return carry

    lax.fori_loop(0, nh, body, 0)


def _wkv_step(state, vecs):
    nb = state.shape[0]
    nh_total = nb * H_B
    nh = 64
    sblk = pl.BlockSpec((nh, DH_B, DH_B), lambda i: (i, 0, 0))
    vblk = pl.BlockSpec((nh, 1, DH_B), lambda i: (i, 0, 0))
    y, s_new = pl.pallas_call(
        functools.partial(_wkv_step_kernel, nh=nh),
        grid=(nh_total // nh,),
        in_specs=[sblk] + [vblk] * 7,
        out_specs=[vblk, sblk],
        out_shape=[jax.ShapeDtypeStruct((nh_total, 1, DH_B), F32),
                   jax.ShapeDtypeStruct((nh_total, DH_B, DH_B), F32)],
        compiler_params=_params(("parallel",)),
        name="wkv_step",
    )(state.reshape(nh_total, DH_B, DH_B), *[x.reshape(nh_total, 1, DH_B) for x in vecs])
    return y.reshape(nb, D_B), s_new.reshape(nb, H_B, DH_B, DH_B)


def _outproj_kernel(ya_ref, y_ref, bv_ref, g_ref, lnw_ref, lnb_ref, ones_ref, h_ref, wo_ref, gpost_ref,
                    o_ref):
    ones = ones_ref[...]
    y = y_ref[...]
    mean = _segsum(y, ones) * (1.0 / DH_B)
    d = y - mean
    var = _segsum(d * d, ones) * (1.0 / DH_B)
    yn = d * lax.rsqrt(var + GN_EPS) * lnw_ref[...] + lnb_ref[...]
    yb = ((yn + bv_ref[...]) * g_ref[...]).astype(BF16)
    mix = _dot(ya_ref[...], wo_ref[0:D_A, :]) + _dot(yb, wo_ref[D_A:, :])
    o_ref[...] = h_ref[...] + _rms(mix, gpost_ref[...])


def _outproj(ya, y, bv, g, ln_w, ln_b, ones, h, w_out, g_post, tm):
    m = h.shape[0]
    row = lambda w: pl.BlockSpec((tm, w), lambda i: (i, 0))
    const = lambda a: pl.BlockSpec(a.shape, lambda i: (0,) * a.ndim)
    return pl.pallas_call(
        _outproj_kernel,
        grid=(m // tm,),
        in_specs=[row(D_A), row(D_B), row(D_B), row(D_B), const(ln_w), const(ln_b), const(ones),
                  row(D_MODEL), const(w_out), const(g_post)],
        out_specs=row(D_MODEL),
        out_shape=jax.ShapeDtypeStruct((m, D_MODEL), F32),
        compiler_params=_params(("parallel",)),
        name="outproj",
    )(ya, y, bv, g, ln_w, ln_b, ones, h, w_out, g_post)


def _block_ones(n, seg):
    i = jnp.arange(n) // seg
    return (i[:, None] == i[None, :]).astype(BF16)


def kernel(x_prompt, x_sample, cache_k, cache_v, state_wkv, state_shift, page_table, n_ffn1_pre, n_ffn1_post, ffn1_gate, ffn1_up, ffn1_down, n_mix_pre, n_mix_post, w_in, w_out, lambda_q1, lambda_k1, lambda_q2, lambda_k2, subln, mu_shift, w0, w2, a0, a2, g2, k_k, k_a, r_k, ln_x_w, ln_x_b, n_ffn2_pre, n_ffn2_post, ffn2_gate, ffn2_up, ffn2_down):
    n_p, t_p, _ = x_prompt.shape
    n_s, t_s, _ = x_sample.shape
    assert t_s == 1
    depth = w_in.shape[0]
    n_pages = page_table.shape[1]
    past_len = n_pages * PAGE_SIZE
    ones_seg = _block_ones(D_B, DH_B)
    ones_pair = _block_ones(LANES, DH_B)
    ones_pair2 = jnp.concatenate([ones_pair, ones_pair], axis=0)
    tab_p = _rope_tables(jnp.arange(t_p, dtype=jnp.int32))
    tab_s = _rope_tables(jnp.full((n_s,), past_len, jnp.int32))
    zeros_lora = jnp.zeros((LORA_W, D_B), F32)
    tm_p = 1024
    tm_proj = 512

    yp = x_prompt.reshape(n_p * t_p, D_MODEL)
    ys = x_sample.reshape(n_s, D_MODEL)
    outs = [[] for _ in range(8)]
    for l in range(depth):
        lam_init = 0.8 - 0.6 * math.exp(-0.3 * l)
        vec = lambda a: a[l].reshape(1, -1)
        ffn1 = (vec(n_ffn1_pre), vec(n_ffn1_post), ffn1_gate[l].astype(BF16), ffn1_up[l].astype(BF16),
                ffn1_down[l].astype(BF16))
        ffn2 = (vec(n_ffn2_pre), vec(n_ffn2_post), ffn2_gate[l].astype(BF16), ffn2_up[l].astype(BF16),
                ffn2_down[l].astype(BF16))
        w_in_b = w_in[l].astype(BF16)
        w_out_b = w_out[l].astype(BF16)
        lams = (vec(lambda_q1), vec(lambda_k1), vec(lambda_q2), vec(lambda_k2))
        prep_w = (vec(mu_shift), vec(w0), vec(a0),
                  jnp.concatenate([w2[l], zeros_lora], axis=0).astype(BF16),
                  jnp.concatenate([zeros_lora, a2[l]], axis=0).astype(BF16),
                  g2[l].astype(BF16), vec(k_k), vec(k_a), r_k[l].reshape(1, D_B), ones_seg)

        def mix_tail(h, ya, y, bv, g, tm):
            h2 = _outproj(ya, y, bv, g, vec(ln_x_w), vec(ln_x_b), ones_seg, h, w_out_b, vec(n_mix_post), tm)
            return _ffn(h2, *ffn2, tm)

        h = _ffn(yp, *ffn1, tm_p)
        q, k, v, kb, vb, pb = _proj(h, vec(n_mix_pre), w_in_b, tab_p, tm_proj, t_p // tm_proj)
        ya = _attn_prompt(q, kb, vb, lams, vec(subln), n_p, t_p, lam_init)
        prev0 = jnp.zeros((n_p, 1, SHIFT_DIM), F32)
        kk_, w_, b_, k2_, v_, c_, kr_, bv_, g_ = _rwkv_prep(pb, prev0, prep_w, tm_proj, t_p)
        y, s_new = _rwkv_scan((kk_, w_, b_, k2_, v_, c_, kr_), ones_pair2, n_p, t_p)
        yp = mix_tail(h, ya, y.reshape(n_p * t_p, D_B), bv_, g_, tm_p)
        outs[0].append(k.reshape(n_p, t_p, H_A, 2 * DH_A))
        outs[1].append(v.reshape(n_p, t_p, H_A, 2 * DH_A))
        outs[2].append(s_new)
        outs[3].append(pb.reshape(n_p, t_p, SHIFT_DIM)[:, -1])

        h = _ffn(ys, *ffn1, n_s)
        q, k, v, kb, vb, pb = _proj(h, vec(n_mix_pre), w_in_b, tab_s, n_s, 1)
        ya = _attn_decode(q, k, v, cache_k, cache_v, l, page_table, lams, vec(subln), lam_init)
        kk_, w_, b_, k2_, v_, c_, kr_, bv_, g_ = _rwkv_prep(pb, state_shift[l], prep_w, n_s, 1)
        y, s_new = _wkv_step(state_wkv[l], (kk_, w_, b_, k2_, v_, c_, kr_))
        ys = mix_tail(h, ya, y, bv_, g_, n_s)
        outs[4].append(k.reshape(n_s, 1, H_A, 2 * DH_A))
        outs[5].append(v.reshape(n_s, 1, H_A, 2 * DH_A))
        outs[6].append(s_new)
        outs[7].append(pb)

    return (yp.reshape(n_p, t_p, D_MODEL), ys.reshape(n_s, 1, D_MODEL),
            *[jnp.stack(o) for o in outs])
```

```python
import functools
import math

import jax
import jax.numpy as jnp
from jax import lax
from jax.experimental import pallas as pl
from jax.experimental.pallas import tpu as pltpu

F32 = jnp.float32
BF16 = jnp.bfloat16

D_MODEL = 1024
H_A = 4
DH_A = 64
D_A = H_A * 2 * DH_A
ROT_DIM = DH_A // 4
ROPE_THETA = 500000.0
H_B = 8
DH_B = 64
D_B = H_B * DH_B
LORA_W = 64
LORA_A = 64
LORA_G = 128
SHIFT_DIM = 3 * D_B + LORA_W + LORA_A + LORA_G
D_IN = 3 * D_A + SHIFT_DIM
D_FF = 2816
PAGE_SIZE = 128
NORM_EPS = 1e-6
GN_EPS = 64e-5

LANES = 128
SUBLANES = 8
VMEM_LIMIT = 48 * 1024 * 1024


def _dot(a, b):
    return jnp.dot(a, b, preferred_element_type=F32)


def _dot_nt(a, b):
    return lax.dot_general(a, b, (((1,), (1,)), ((), ())), preferred_element_type=F32)


def _rms(x, g):
    return x * lax.rsqrt(jnp.mean(x * x, axis=-1, keepdims=True) + NORM_EPS) * g


def _sigmoid(x):
    return 1.0 / (1.0 + jnp.exp(-x))


def _split2(x):
    hi = x.astype(BF16)
    mid = (x - hi.astype(F32)).astype(BF16)
    return hi, mid


def _segsum(x, ones):
    hi = x.astype(BF16)
    r1 = x - hi.astype(F32)
    mid = r1.astype(BF16)
    lo = (r1 - mid.astype(F32)).astype(BF16)
    return _dot(hi, ones) + _dot(mid, ones) + _dot(lo, ones)


def _params(sem):
    return pltpu.CompilerParams(dimension_semantics=sem, vmem_limit_bytes=VMEM_LIMIT)


def _ffn_kernel(x_ref, gpre_ref, gpost_ref, wg_ref, wu_ref, wd_ref, o_ref, un_ref, acc_ref, *, nf):
    k = pl.program_id(1)

    @pl.when(k == 0)
    def _():
        un_ref[...] = _rms(x_ref[...], gpre_ref[...]).astype(BF16)
        acc_ref[...] = jnp.zeros_like(acc_ref)

    un = un_ref[...]
    g = _dot(un, wg_ref[...])
    u = _dot(un, wu_ref[...])
    hid = (g * _sigmoid(g)) * u
    acc_ref[...] += _dot(hid.astype(BF16), wd_ref[...])

    @pl.when(k == nf - 1)
    def _():
        o_ref[...] = x_ref[...] + 0.5 * _rms(acc_ref[...], gpost_ref[...])


def _ffn(x, g_pre, g_post, wg, wu, wd, tm):
    m = x.shape[0]
    tf = 256
    nf = D_FF // tf
    row = pl.BlockSpec((tm, D_MODEL), lambda i, k: (i, 0))
    vec = pl.BlockSpec((1, D_MODEL), lambda i, k: (0, 0))
    return pl.pallas_call(
        functools.partial(_ffn_kernel, nf=nf),
        grid=(m // tm, nf),
        in_specs=[row, vec, vec,
                  pl.BlockSpec((D_MODEL, tf), lambda i, k: (0, k)),
                  pl.BlockSpec((D_MODEL, tf), lambda i, k: (0, k)),
                  pl.BlockSpec((tf, D_MODEL), lambda i, k: (k, 0))],
        out_specs=row,
        out_shape=jax.ShapeDtypeStruct((m, D_MODEL), F32),
        scratch_shapes=[pltpu.VMEM((tm, D_MODEL), BF16), pltpu.VMEM((tm, D_MODEL), F32)],
        compiler_params=_params(("parallel", "arbitrary")),
        name="ffn",
    )(x, g_pre, g_post, wg, wu, wd)


def _proj_kernel(h_ref, g_ref, w_ref, cos_ref, sina_ref, sinb_ref,
                 q_ref, k_ref, v_ref, kb_ref, vt_ref, pb_ref):
    u = _rms(h_ref[...], g_ref[...]).astype(BF16)
    cos = cos_ref[...]
    sina = sina_ref[...]
    sinb = sinb_ref[...]
    half = ROT_DIM // 2

    def rope(x):
        return x * cos + pltpu.roll(x, LANES - half, 1) * sina + pltpu.roll(x, half, 1) * sinb

    qa = _dot(u, w_ref[:, 0:D_A])
    ka = _dot(u, w_ref[:, D_A:2 * D_A])
    for hh in range(H_A):
        sl = slice(hh * LANES, (hh + 1) * LANES)
        q_ref[:, sl] = (rope(qa[:, sl]) * (DH_A ** -0.5)).astype(BF16)
        kh = rope(ka[:, sl])
        k_ref[:, sl] = kh
        kb_ref[:, sl] = kh.astype(BF16)
    va = _dot(u, w_ref[:, 2 * D_A:3 * D_A])
    v_ref[...] = va
    vt_ref[...] = va.T.astype(BF16)
    pb_ref[...] = _dot(u, w_ref[:, 3 * D_A:])


def _proj(h, g, w_in, tables, tm, table_blocks):
    m = h.shape[0]
    row = lambda width: pl.BlockSpec((tm, width), lambda i: (i, 0))
    tab = pl.BlockSpec((tm, LANES), lambda i: (i % table_blocks, 0))
    shp = lambda width, dt: jax.ShapeDtypeStruct((m, width), dt)
    return pl.pallas_call(
        _proj_kernel,
        grid=(m // tm,),
        in_specs=[row(D_MODEL), pl.BlockSpec((1, D_MODEL), lambda i: (0, 0)),
                  pl.BlockSpec((D_MODEL, D_IN), lambda i: (0, 0)), tab, tab, tab],
        out_specs=[row(D_A), row(D_A), row(D_A), row(D_A), pl.BlockSpec((D_A, tm), lambda i: (0, i)),
                   row(SHIFT_DIM)],
        out_shape=[shp(D_A, BF16), shp(D_A, F32), shp(D_A, F32), shp(D_A, BF16),
                   jax.ShapeDtypeStruct((D_A, m), BF16), shp(SHIFT_DIM, F32)],
        compiler_params=_params(("parallel",)),
        name="proj",
    )(h, g, w_in, *tables)


def _rope_tables(pos):
    half = ROT_DIM // 2
    t = pos.shape[0]
    inv_freq = ROPE_THETA ** (-jnp.arange(half, dtype=F32) / half)
    ang = pos.astype(F32)[:, None] * inv_freq[None, :]
    cos = jnp.cos(ang)
    sin = jnp.sin(ang)
    rest = DH_A - ROT_DIM
    cos64 = jnp.concatenate([cos, cos, jnp.ones((t, rest), F32)], axis=1)
    sina64 = jnp.concatenate([-sin, jnp.zeros((t, half + rest), F32)], axis=1)
    sinb64 = jnp.concatenate([jnp.zeros((t, half), F32), sin, jnp.zeros((t, rest), F32)], axis=1)
    two = lambda x: jnp.concatenate([x, x], axis=1)
    return two(cos64), two(sina64), two(sinb64)


def _lambda(lq1_ref, lk1_ref, lq2_ref, lk2_ref, lam_init):
    s1 = jnp.sum(lq1_ref[...] * lk1_ref[...], axis=-1, keepdims=True)
    s2 = jnp.sum(lq2_ref[...] * lk2_ref[...], axis=-1, keepdims=True)
    return jnp.exp(s1) - jnp.exp(s2) + lam_init


def _attn_kernel(q_ref, k_ref, vt_ref, lq1_ref, lk1_ref, lq2_ref, lk2_ref, subln_ref, o_ref,
                 m_ref, l_ref, acc_ref, *, tq, lam_init):
    qi = pl.program_id(2)
    q = q_ref[...].astype(F32)
    lane = lax.broadcasted_iota(jnp.int32, q.shape, 1)
    qs = (jnp.where(lane < DH_A, q, 0.0).astype(BF16), jnp.where(lane >= DH_A, q, 0.0).astype(BF16))
    m_ref[...] = jnp.full(m_ref.shape, -jnp.inf, F32)
    l_ref[...] = jnp.zeros_like(l_ref)
    acc_ref[...] = jnp.zeros_like(acc_ref)
    krow = lax.broadcasted_iota(jnp.int32, (tq, tq), 0)
    qcol = lax.broadcasted_iota(jnp.int32, (tq, tq), 1)

    def block(kstart, diagonal):
        k = k_ref[pl.ds(kstart, tq), :]
        vt = vt_ref[:, pl.ds(kstart, tq)]
        for j in range(2):
            st = _dot_nt(k, qs[j])
            if diagonal:
                st = jnp.where(krow <= qcol, st, -jnp.inf)
            m_prev = m_ref[j]
            m_new = jnp.maximum(m_prev, jnp.max(st, axis=0, keepdims=True))
            alpha = jnp.exp(m_prev - m_new)
            p = jnp.exp(st - m_new)
            l_ref[j] = alpha * l_ref[j] + jnp.sum(p, axis=0, keepdims=True)
            acc_ref[j] = alpha * acc_ref[j] + _dot(vt, p.astype(BF16))
            m_ref[j] = m_new

    def body(ki, carry):
        block(pl.multiple_of(ki * tq, tq), False)
        return carry

    lax.fori_loop(0, qi, body, 0)
    block(pl.multiple_of(qi * tq, tq), True)

    lam = _lambda(lq1_ref, lk1_ref, lq2_ref, lk2_ref, lam_init)
    ot = acc_ref[0] / l_ref[0] - lam * (acc_ref[1] / l_ref[1])
    o_ref[...] = (_rms(ot.T, subln_ref[...]) * (1.0 - lam_init)).astype(BF16)


def _attn_prompt(q, kb, vt, lams, subln, n, t, lam_init):
    tq = 512
    nq = t // tq
    qspec = pl.BlockSpec((tq, LANES), lambda b, h, i: (b * nq + i, h))
    kspec = pl.BlockSpec((t, LANES), lambda b, h, i: (b, h))
    vtspec = pl.BlockSpec((LANES, t), lambda b, h, i: (h, b))
    small = lambda w: pl.BlockSpec((1, w), lambda b, h, i: (0, 0))
    return pl.pallas_call(
        functools.partial(_attn_kernel, tq=tq, lam_init=lam_init),
        grid=(n, H_A, nq),
        in_specs=[qspec, kspec, vtspec, small(DH_A), small(DH_A), small(DH_A), small(DH_A),
                  small(2 * DH_A)],
        out_specs=qspec,
        out_shape=jax.ShapeDtypeStruct((n * t, D_A), BF16),
        scratch_shapes=[pltpu.VMEM((2, 1, tq), F32), pltpu.VMEM((2, 1, tq), F32),
                        pltpu.VMEM((2, LANES, tq), F32)],
        compiler_params=_params(("parallel", "parallel", "arbitrary")),
        name="attn_prompt",
    )(q, kb, vt, *lams, subln)


def _attn_decode_kernel(pt_ref, q_ref, ks_ref, vs_ref, lq1_ref, lk1_ref, lq2_ref, lk2_ref, subln_ref,
                        *rest, n_pages, lam_init):
    del pt_ref
    kp_refs = rest[:n_pages]
    vp_refs = rest[n_pages:2 * n_pages]
    o_ref = rest[2 * n_pages]
    nmap = 2 * H_A
    page_rows = PAGE_SIZE * H_A
    heads = lambda x: jnp.concatenate([x[:, hh * LANES:(hh + 1) * LANES] for hh in range(H_A)], axis=0)
    q4 = heads(q_ref[0].astype(F32))
    k4 = heads(ks_ref[0].astype(BF16).astype(F32))
    v4 = heads(vs_ref[0].astype(BF16).astype(F32))
    r8 = lax.broadcasted_iota(jnp.int32, (nmap, LANES), 0)
    l8 = lax.broadcasted_iota(jnp.int32, (nmap, LANES), 1)
    q8 = jnp.where((l8 // DH_A) == (r8 // H_A), jnp.concatenate([q4, q4], axis=0), 0.0)
    q8_b = q8.astype(BF16)
    s = jnp.concatenate([_dot_nt(q8_b, kp_refs[pg][...].astype(BF16)) for pg in range(n_pages)], axis=1)
    rs = lax.broadcasted_iota(jnp.int32, s.shape, 0)
    cs = lax.broadcasted_iota(jnp.int32, s.shape, 1)
    s = jnp.where((cs % H_A) == (rs % H_A), s, -jnp.inf)
    s_self = jnp.sum(q8 * jnp.concatenate([k4, k4], axis=0), axis=1, keepdims=True)
    m = jnp.maximum(jnp.max(s, axis=1, keepdims=True), s_self)
    e = jnp.exp(s - m)
    e_self = jnp.exp(s_self - m)
    inv = 1.0 / (jnp.sum(e, axis=1, keepdims=True) + e_self)
    lam = _lambda(lq1_ref, lk1_ref, lq2_ref, lk2_ref, lam_init)
    p = e * inv
    p_self = e_self * inv
    pc = (p[0:H_A] - lam * p[H_A:nmap]).astype(BF16)
    pc_self = (p_self[0:H_A] - lam * p_self[H_A:nmap]).astype(BF16).astype(F32)
    o = pc_self * v4
    for pg in range(n_pages):
        o = o + _dot(pc[:, pg * page_rows:(pg + 1) * page_rows], vp_refs[pg][...].astype(BF16))
    o = (_rms(o, subln_ref[...]) * (1.0 - lam_init)).astype(BF16)
    for hh in range(H_A):
        o_ref[0, :, hh * LANES:(hh + 1) * LANES] = o[hh:hh + 1, :]


def _attn_decode(q, k_self, v_self, cache_k, cache_v, layer, page_table, lams, subln, lam_init):
    nb, n_pages = page_table.shape
    tok = pl.BlockSpec((1, 1, D_A), lambda b, pt: (b, 0, 0))
    small = lambda w: pl.BlockSpec((1, w), lambda b, pt: (0, 0))
    as_rows = lambda c: c.reshape(c.shape[0], c.shape[1], PAGE_SIZE * H_A, 2 * DH_A)
    cache_k, cache_v = as_rows(cache_k), as_rows(cache_v)
    page = lambda p: pl.BlockSpec((None, None, PAGE_SIZE * H_A, 2 * DH_A),
                                  lambda b, pt: (layer, pt[b, p], 0, 0))
    pages = [page(p) for p in range(n_pages)]
    grid_spec = pltpu.PrefetchScalarGridSpec(
        num_scalar_prefetch=1,
        grid=(nb,),
        in_specs=[tok, tok, tok, small(DH_A), small(DH_A), small(DH_A), small(DH_A), small(2 * DH_A)]
        + pages + pages,
        out_specs=tok,
    )
    out = pl.pallas_call(
        functools.partial(_attn_decode_kernel, n_pages=n_pages, lam_init=lam_init),
        grid_spec=grid_spec,
        out_shape=jax.ShapeDtypeStruct((nb, 1, D_A), BF16),
        compiler_params=_params(("arbitrary",)),
        name="attn_decode",
    )(page_table, q.reshape(nb, 1, D_A), k_self.reshape(nb, 1, D_A), v_self.reshape(nb, 1, D_A),
      *lams, subln, *([cache_k] * n_pages), *([cache_v] * n_pages))
    return out.reshape(nb, D_A)


def _prep_kernel(*refs, tm, tiles_per_seq):
    if tiles_per_seq:
        pb_ref, prev_ref, tail_ref = refs[:3]
        refs = refs[3:]
    else:
        pb_ref, prev_ref = refs[:2]
        refs = refs[2:]
    (mu_ref, w0_ref, a0_ref, w2_ref, a2_ref, g2_ref, kk_w_ref, ka_w_ref, rk_w_ref, ones_ref,
     kk_o, w_o, b_o, k_o, v_o, c_o, kr_o, bv_o, g_o) = refs
    pb = pb_ref[...]
    if tiles_per_seq:
        first = (pl.program_id(0) % tiles_per_seq) == 0
        prev_row = jnp.where(first, prev_ref[0], tail_ref[SUBLANES - 1:SUBLANES, :])
        rows = lax.broadcasted_iota(jnp.int32, pb.shape, 0)
        shifted = jnp.where(rows == 0, jnp.broadcast_to(prev_row, pb.shape), pltpu.roll(pb, 1, 0))
    else:
        shifted = prev_ref[...]
    xs = pb + (shifted - pb) * mu_ref[...]
    r = xs[:, 0:D_B]
    k = xs[:, D_B:2 * D_B]
    v = xs[:, 2 * D_B:3 * D_B]
    xwa = xs[:, 3 * D_B:3 * D_B + LORA_W + LORA_A]
    xg = xs[:, 3 * D_B + LORA_W + LORA_A:]
    ones = ones_ref[...]
    w_raw = w0_ref[...] + _dot(jnp.tanh(xwa).astype(BF16), w2_ref[...])
    z = -w_raw
    softplus = jnp.maximum(z, 0.0) + jnp.log(1.0 + jnp.exp(-jnp.abs(z)))
    decay = jnp.exp(-jnp.exp(-softplus - 0.5))
    a = _sigmoid(a0_ref[...] + _dot(xwa.astype(BF16), a2_ref[...]))
    g = _dot(_sigmoid(xg).astype(BF16), g2_ref[...])
    kk = k * kk_w_ref[...]
    kk = kk / jnp.maximum(jnp.sqrt(_segsum(kk * kk, ones)), 1e-12)
    k2 = k * (1.0 + (a - 1.0) * ka_w_ref[...])
    b = kk * a
    br = _segsum(b * r, ones)
    kr = _segsum(k2 * r, ones)
    bonus = _segsum(r * k2 * rk_w_ref[...], ones)
    kk_o[...] = kk
    w_o[...] = decay
    b_o[...] = b
    k_o[...] = k2
    v_o[...] = v
    c_o[...] = decay * r - kk * br
    kr_o[...] = kr
    bv_o[...] = bonus * v
    g_o[...] = g


def _rwkv_prep(pb, prev, weights, tm, seq_len):
    m = pb.shape[0]
    row = lambda w: pl.BlockSpec((tm, w), lambda i: (i, 0))
    const = lambda a: pl.BlockSpec(a.shape, lambda i: (0,) * a.ndim)
    if seq_len > 1:
        tiles_per_seq = seq_len // tm
        tail = pl.BlockSpec((SUBLANES, SHIFT_DIM),
                            lambda i: (jnp.maximum(i * (tm // SUBLANES) - 1, 0), 0))
        head = [row(SHIFT_DIM), pl.BlockSpec((1, 1, SHIFT_DIM), lambda i: (i // tiles_per_seq, 0, 0)), tail]
        args = [pb, prev, pb]
    else:
        tiles_per_seq = 0
        head = [row(SHIFT_DIM), row(SHIFT_DIM)]
        args = [pb, prev]
    return pl.pallas_call(
        functools.partial(_prep_kernel, tm=tm, tiles_per_seq=tiles_per_seq),
        grid=(m // tm,),
        in_specs=head + [const(a) for a in weights],
        out_specs=[row(D_B)] * 9,
        out_shape=[jax.ShapeDtypeStruct((m, D_B), F32)] * 9,
        compiler_params=_params(("parallel",)),
        name="rwkv_prep",
    )(*args, *weights)


PAIRS = H_B // 2
STEP_UNROLL = SUBLANES


def _scan_kernel(kk_ref, w_ref, b_ref, k_ref, v_ref, c_ref, kr_ref, ones_ref, y_ref, sout_ref, s_ref,
                 *, nb, tb):
    step_blk = pl.program_id(0)

    @pl.when(step_blk == 0)
    def _():
        s_ref[...] = jnp.zeros_like(s_ref)

    ones = ones_ref[...]
    sub = lax.broadcasted_iota(jnp.int32, (DH_B, LANES), 0)
    lan = lax.broadcasted_iota(jnp.int32, (DH_B, LANES), 1)
    diag = (sub == (lan % DH_B)).astype(F32)

    def trip(g, carry):
        t0 = pl.multiple_of(g * STEP_UNROLL, STEP_UNROLL)
        tiles = {}
        for n in range(nb):
            for p in range(PAIRS):
                sl = slice(p * LANES, (p + 1) * LANES)
                tiles[n, p] = tuple(ref[n, pl.ds(t0, STEP_UNROLL), sl]
                                    for ref in (kk_ref, w_ref, b_ref, k_ref, v_ref, c_ref, kr_ref))
        y_rows = {key: [] for key in tiles}
        for j in range(STEP_UNROLL):
            lhs = []
            for n in range(nb):
                for p in range(PAIRS):
                    kk, _, _, _, v, c, _ = (x[j:j + 1, :] for x in tiles[n, p])
                    s = s_ref[n * PAIRS + p]
                    for part in (s * kk, s * c, diag * v):
                        hi, mid = _split2(part)
                        lhs.append(jnp.concatenate([hi, mid], axis=1))
            red = _dot(jnp.concatenate(lhs, axis=0), ones)
            for n in range(nb):
                for p in range(PAIRS):
                    idx = n * PAIRS + p
                    _, w, b, k, v, _, kr = (x[j:j + 1, :] for x in tiles[n, p])
                    base = idx * 3 * DH_B
                    sa = -red[base:base + DH_B]
                    z = red[base + DH_B:base + 2 * DH_B]
                    vb = red[base + 2 * DH_B:base + 3 * DH_B]
                    s_ref[idx] = s_ref[idx] * w + sa * b + vb * k
                    y_rows[n, p].append(jnp.sum(diag * z, axis=0, keepdims=True) + v * kr)
        for n in range(nb):
            for p in range(PAIRS):
                y_ref[n, pl.ds(t0, STEP_UNROLL), p * LANES:(p + 1) * LANES] = (
                    jnp.concatenate(y_rows[n, p], axis=0))
        return carry

    lax.fori_loop(0, tb // STEP_UNROLL, trip, 0)

    @pl.when(step_blk == pl.num_programs(0) - 1)
    def _():
        for n in range(nb):
            for p in range(PAIRS):
                s = s_ref[n * PAIRS + p]
                sout_ref[n, 2 * p] = s[:, :DH_B]
                sout_ref[n, 2 * p + 1] = s[:, DH_B:]


def _rwkv_scan(vecs, ones2, n, t):
    tb = 128
    blk = pl.BlockSpec((n, tb, D_B), lambda i: (0, i, 0))
    return pl.pallas_call(
        functools.partial(_scan_kernel, nb=n, tb=tb),
        grid=(t // tb,),
        in_specs=[blk] * 7 + [pl.BlockSpec(ones2.shape, lambda i: (0, 0))],
        out_specs=[blk, pl.BlockSpec((n, H_B, DH_B, DH_B), lambda i: (0, 0, 0, 0))],
        out_shape=[jax.ShapeDtypeStruct((n, t, D_B), F32),
                   jax.ShapeDtypeStruct((n, H_B, DH_B, DH_B), F32)],
        scratch_shapes=[pltpu.VMEM((n * PAIRS, DH_B, LANES), F32)],
        compiler_params=_params(("arbitrary",)),
        name="rwkv_scan",
    )(*[x.reshape(n, t, D_B) for x in vecs], ones2)


def _wkv_step_kernel(s_ref, kk_ref, w_ref, b_ref, k_ref, v_ref, c_ref, kr_ref, y_ref, so_ref, *, nh):
    sub = lax.broadcasted_iota(jnp.int32, (DH_B, DH_B), 0)
    lan = lax.broadcasted_iota(jnp.int32, (DH_B, DH_B), 1)
    diag = (sub == lan).astype(F32)

    def body(i, carry):
        s = s_ref[i]
        v = v_ref[i]
        sa = -jnp.sum(s * kk_ref[i], axis=1, keepdims=True)
        z = jnp.sum(s * c_ref[i], axis=1, keepdims=True)
        vcol = jnp.sum(diag * v, axis=1, keepdims=True)
        so_ref[i] = s * w_ref[i] + sa * b_ref[i] + vcol * k_ref[i]
        y_ref[i] = jnp.sum(diag * z, axis=0, keepdims=True) + v * kr_ref[i]
        return carry

    lax.fori_loop(0, nh, body, 0, unroll=8)


def _wkv_step(state, vecs):
    nb = state.shape[0]
    nh_total = nb * H_B
    nh = 64
    sblk = pl.BlockSpec((nh, DH_B, DH_B), lambda i: (i, 0, 0))
    vblk = pl.BlockSpec((nh, 1, DH_B), lambda i: (i, 0, 0))
    y, s_new = pl.pallas_call(
        functools.partial(_wkv_step_kernel, nh=nh),
        grid=(nh_total // nh,),
        in_specs=[sblk] + [vblk] * 7,
        out_specs=[vblk, sblk],
        out_shape=[jax.ShapeDtypeStruct((nh_total, 1, DH_B), F32),
                   jax.ShapeDtypeStruct((nh_total, DH_B, DH_B), F32)],
        compiler_params=_params(("parallel",)),
        name="wkv_step",
    )(state.reshape(nh_total, DH_B, DH_B), *[x.reshape(nh_total, 1, DH_B) for x in vecs])
    return y.reshape(nb, D_B), s_new.reshape(nb, H_B, DH_B, DH_B)


def _outproj_kernel(ya_ref, y_ref, bv_ref, g_ref, lnw_ref, lnb_ref, ones_ref, h_ref, wo_ref, gpost_ref,
                    o_ref):
    ones = ones_ref[...]
    y = y_ref[...]
    mean = _segsum(y, ones) * (1.0 / DH_B)
    d = y - mean
    var = _segsum(d * d, ones) * (1.0 / DH_B)
    yn = d * lax.rsqrt(var + GN_EPS) * lnw_ref[...] + lnb_ref[...]
    yb = ((yn + bv_ref[...]) * g_ref[...]).astype(BF16)
    mix = _dot(ya_ref[...], wo_ref[0:D_A, :]) + _dot(yb, wo_ref[D_A:, :])
    o_ref[...] = h_ref[...] + _rms(mix, gpost_ref[...])


def _outproj(ya, y, bv, g, ln_w, ln_b, ones, h, w_out, g_post, tm):
    m = h.shape[0]
    row = lambda w: pl.BlockSpec((tm, w), lambda i: (i, 0))
    const = lambda a: pl.BlockSpec(a.shape, lambda i: (0,) * a.ndim)
    return pl.pallas_call(
        _outproj_kernel,
        grid=(m // tm,),
        in_specs=[row(D_A), row(D_B), row(D_B), row(D_B), const(ln_w), const(ln_b), const(ones),
                  row(D_MODEL), const(w_out), const(g_post)],
        out_specs=row(D_MODEL),
        out_shape=jax.ShapeDtypeStruct((m, D_MODEL), F32),
        compiler_params=_params(("parallel",)),
        name="outproj",
    )(ya, y, bv, g, ln_w, ln_b, ones, h, w_out, g_post)


def _block_ones(n, seg):
    i = jnp.arange(n) // seg
    return (i[:, None] == i[None, :]).astype(BF16)


def kernel(x_prompt, x_sample, cache_k, cache_v, state_wkv, state_shift, page_table, n_ffn1_pre, n_ffn1_post, ffn1_gate, ffn1_up, ffn1_down, n_mix_pre, n_mix_post, w_in, w_out, lambda_q1, lambda_k1, lambda_q2, lambda_k2, subln, mu_shift, w0, w2, a0, a2, g2, k_k, k_a, r_k, ln_x_w, ln_x_b, n_ffn2_pre, n_ffn2_post, ffn2_gate, ffn2_up, ffn2_down):
    n_p, t_p, _ = x_prompt.shape
    n_s, t_s, _ = x_sample.shape
    assert t_s == 1
    depth = w_in.shape[0]
    n_pages = page_table.shape[1]
    past_len = n_pages * PAGE_SIZE
    ones_seg = _block_ones(D_B, DH_B)
    ones_pair = _block_ones(LANES, DH_B)
    ones_pair2 = jnp.concatenate([ones_pair, ones_pair], axis=0)
    tab_p = _rope_tables(jnp.arange(t_p, dtype=jnp.int32))
    tab_s = _rope_tables(jnp.full((n_s,), past_len, jnp.int32))
    zeros_lora = jnp.zeros((LORA_W, D_B), F32)
    tm_p = 1024
    tm_proj = 512

    yp = x_prompt.reshape(n_p * t_p, D_MODEL)
    ys = x_sample.reshape(n_s, D_MODEL)
    outs = [[] for _ in range(8)]
    for l in range(depth):
        lam_init = 0.8 - 0.6 * math.exp(-0.3 * l)
        vec = lambda a: a[l].reshape(1, -1)
        ffn1 = (vec(n_ffn1_pre), vec(n_ffn1_post), ffn1_gate[l].astype(BF16), ffn1_up[l].astype(BF16),
                ffn1_down[l].astype(BF16))
        ffn2 = (vec(n_ffn2_pre), vec(n_ffn2_post), ffn2_gate[l].astype(BF16), ffn2_up[l].astype(BF16),
                ffn2_down[l].astype(BF16))
        w_in_b = w_in[l].astype(BF16)
        w_out_b = w_out[l].astype(BF16)
        lams = (vec(lambda_q1), vec(lambda_k1), vec(lambda_q2), vec(lambda_k2))
        prep_w = (vec(mu_shift), vec(w0), vec(a0),
                  jnp.concatenate([w2[l], zeros_lora], axis=0).astype(BF16),
                  jnp.concatenate([zeros_lora, a2[l]], axis=0).astype(BF16),
                  g2[l].astype(BF16), vec(k_k), vec(k_a), r_k[l].reshape(1, D_B), ones_seg)

        def mix_tail(h, ya, y, bv, g, tm):
            h2 = _outproj(ya, y, bv, g, vec(ln_x_w), vec(ln_x_b), ones_seg, h, w_out_b, vec(n_mix_post), tm)
            return _ffn(h2, *ffn2, tm)

        h = _ffn(yp, *ffn1, tm_p)
        q, k, v, kb, vt, pb = _proj(h, vec(n_mix_pre), w_in_b, tab_p, tm_proj, t_p // tm_proj)
        ya = _attn_prompt(q, kb, vt, lams, vec(subln), n_p, t_p, lam_init)
        prev0 = jnp.zeros((n_p, 1, SHIFT_DIM), F32)
        kk_, w_, b_, k2_, v_, c_, kr_, bv_, g_ = _rwkv_prep(pb, prev0, prep_w, tm_proj, t_p)
        y, s_new = _rwkv_scan((kk_, w_, b_, k2_, v_, c_, kr_), ones_pair2, n_p, t_p)
        yp = mix_tail(h, ya, y.reshape(n_p * t_p, D_B), bv_, g_, tm_p)
        outs[0].append(k.reshape(n_p, t_p, H_A, 2 * DH_A))
        outs[1].append(v.reshape(n_p, t_p, H_A, 2 * DH_A))
        outs[2].append(s_new)
        outs[3].append(pb.reshape(n_p, t_p, SHIFT_DIM)[:, -1])

        h = _ffn(ys, *ffn1, n_s)
        q, k, v, _, _, pb = _proj(h, vec(n_mix_pre), w_in_b, tab_s, n_s, 1)
        ya = _attn_decode(q, k, v, cache_k, cache_v, l, page_table, lams, vec(subln), lam_init)
        kk_, w_, b_, k2_, v_, c_, kr_, bv_, g_ = _rwkv_prep(pb, state_shift[l], prep_w, n_s, 1)
        y, s_new = _wkv_step(state_wkv[l], (kk_, w_, b_, k2_, v_, c_, kr_))
        ys = mix_tail(h, ya, y, bv_, g_, n_s)
        outs[4].append(k.reshape(n_s, 1, H_A, 2 * DH_A))
        outs[5].append(v.reshape(n_s, 1, H_A, 2 * DH_A))
        outs[6].append(s_new)
        outs[7].append(pb)

    return (yp.reshape(n_p, t_p, D_MODEL), ys.reshape(n_s, 1, D_MODEL),
            *[jnp.stack(o) for o in outs])
```

```python
import functools
import math

import jax
import jax.numpy as jnp
from jax import lax
from jax.experimental import pallas as pl
from jax.experimental.pallas import tpu as pltpu

F32 = jnp.float32
BF16 = jnp.bfloat16

D_MODEL = 1024
H_A = 4
DH_A = 64
D_A = H_A * 2 * DH_A
ROT_DIM = DH_A // 4
ROPE_THETA = 500000.0
H_B = 8
DH_B = 64
D_B = H_B * DH_B
LORA_W = 64
LORA_A = 64
LORA_G = 128
SHIFT_DIM = 3 * D_B + LORA_W + LORA_A + LORA_G
D_IN = 3 * D_A + SHIFT_DIM
D_FF = 2816
PAGE_SIZE = 128
NORM_EPS = 1e-6
GN_EPS = 64e-5

LANES = 128
SUBLANES = 8
VMEM_LIMIT = 48 * 1024 * 1024


def _dot(a, b):
    return jnp.dot(a, b, preferred_element_type=F32)


def _dot_nt(a, b):
    return lax.dot_general(a, b, (((1,), (1,)), ((), ())), preferred_element_type=F32)


def _rms(x, g):
    return x * lax.rsqrt(jnp.mean(x * x, axis=-1, keepdims=True) + NORM_EPS) * g


def _sigmoid(x):
    return 1.0 / (1.0 + jnp.exp(-x))


def _split2(x):
    hi = x.astype(BF16)
    mid = (x - hi.astype(F32)).astype(BF16)
    return hi, mid


def _segsum(x, ones):
    hi = x.astype(BF16)
    r1 = x - hi.astype(F32)
    mid = r1.astype(BF16)
    lo = (r1 - mid.astype(F32)).astype(BF16)
    return _dot(hi, ones) + _dot(mid, ones) + _dot(lo, ones)


def _params(sem):
    return pltpu.CompilerParams(dimension_semantics=sem, vmem_limit_bytes=VMEM_LIMIT)


def _ffn_kernel(x_ref, gpre_ref, gpost_ref, wg_ref, wu_ref, wd_ref, o_ref):
    x = x_ref[...]
    un = _rms(x, gpre_ref[...]).astype(BF16)
    g = _dot(un, wg_ref[...])
    u = _dot(un, wu_ref[...])
    hid = ((g * _sigmoid(g)) * u).astype(BF16)
    o_ref[...] = x + 0.5 * _rms(_dot(hid, wd_ref[...]), gpost_ref[...])


def _ffn(x, g_pre, g_post, wg, wu, wd, tm):
    m = x.shape[0]
    row = pl.BlockSpec((tm, D_MODEL), lambda i: (i, 0))
    vec = pl.BlockSpec((1, D_MODEL), lambda i: (0, 0))
    resident = lambda w: pl.BlockSpec(w.shape, lambda i: (0, 0), pipeline_mode=pl.Buffered(1))
    return pl.pallas_call(
        _ffn_kernel,
        grid=(m // tm,),
        in_specs=[row, vec, vec, resident(wg), resident(wu), resident(wd)],
        out_specs=row,
        out_shape=jax.ShapeDtypeStruct((m, D_MODEL), F32),
        compiler_params=_params(("parallel",)),
        name="ffn",
    )(x, g_pre, g_post, wg, wu, wd)


def _proj_kernel(h_ref, g_ref, w_ref, cos_ref, sina_ref, sinb_ref,
                 q_ref, k_ref, v_ref, kb_ref, vt_ref, pb_ref):
    u = _rms(h_ref[...], g_ref[...]).astype(BF16)
    cos = cos_ref[...]
    sina = sina_ref[...]
    sinb = sinb_ref[...]
    half = ROT_DIM // 2

    def rope(x):
        return x * cos + pltpu.roll(x, LANES - half, 1) * sina + pltpu.roll(x, half, 1) * sinb

    qa = _dot(u, w_ref[:, 0:D_A])
    ka = _dot(u, w_ref[:, D_A:2 * D_A])
    for hh in range(H_A):
        sl = slice(hh * LANES, (hh + 1) * LANES)
        q_ref[:, sl] = (rope(qa[:, sl]) * (DH_A ** -0.5)).astype(BF16)
        kh = rope(ka[:, sl])
        k_ref[:, sl] = kh
        kb_ref[:, sl] = kh.astype(BF16)
    va = _dot(u, w_ref[:, 2 * D_A:3 * D_A])
    v_ref[...] = va
    vt_ref[...] = va.T.astype(BF16)
    pb_ref[...] = _dot(u, w_ref[:, 3 * D_A:])


def _proj(h, g, w_in, tables, tm, table_blocks):
    m = h.shape[0]
    row = lambda width: pl.BlockSpec((tm, width), lambda i: (i, 0))
    tab = pl.BlockSpec((tm, LANES), lambda i: (i % table_blocks, 0))
    shp = lambda width, dt: jax.ShapeDtypeStruct((m, width), dt)
    return pl.pallas_call(
        _proj_kernel,
        grid=(m // tm,),
        in_specs=[row(D_MODEL), pl.BlockSpec((1, D_MODEL), lambda i: (0, 0)),
                  pl.BlockSpec((D_MODEL, D_IN), lambda i: (0, 0)), tab, tab, tab],
        out_specs=[row(D_A), row(D_A), row(D_A), row(D_A), pl.BlockSpec((D_A, tm), lambda i: (0, i)),
                   row(SHIFT_DIM)],
        out_shape=[shp(D_A, BF16), shp(D_A, F32), shp(D_A, F32), shp(D_A, BF16),
                   jax.ShapeDtypeStruct((D_A, m), BF16), shp(SHIFT_DIM, F32)],
        compiler_params=_params(("parallel",)),
        name="proj",
    )(h, g, w_in, *tables)


def _rope_tables(pos):
    half = ROT_DIM // 2
    t = pos.shape[0]
    inv_freq = ROPE_THETA ** (-jnp.arange(half, dtype=F32) / half)
    ang = pos.astype(F32)[:, None] * inv_freq[None, :]
    cos = jnp.cos(ang)
    sin = jnp.sin(ang)
    rest = DH_A - ROT_DIM
    cos64 = jnp.concatenate([cos, cos, jnp.ones((t, rest), F32)], axis=1)
    sina64 = jnp.concatenate([-sin, jnp.zeros((t, half + rest), F32)], axis=1)
    sinb64 = jnp.concatenate([jnp.zeros((t, half), F32), sin, jnp.zeros((t, rest), F32)], axis=1)
    two = lambda x: jnp.concatenate([x, x], axis=1)
    return two(cos64), two(sina64), two(sinb64)


def _lambda(lq1_ref, lk1_ref, lq2_ref, lk2_ref, lam_init):
    s1 = jnp.sum(lq1_ref[...] * lk1_ref[...], axis=-1, keepdims=True)
    s2 = jnp.sum(lq2_ref[...] * lk2_ref[...], axis=-1, keepdims=True)
    return jnp.exp(s1) - jnp.exp(s2) + lam_init


def _attn_kernel(q_ref, k_ref, vt_ref, lq1_ref, lk1_ref, lq2_ref, lk2_ref, subln_ref, o_ref,
                 m_ref, l_ref, acc_ref, *, tq, lam_init):
    qi = pl.program_id(2)
    q = q_ref[...].astype(F32)
    lane = lax.broadcasted_iota(jnp.int32, q.shape, 1)
    qs = (jnp.where(lane < DH_A, q, 0.0).astype(BF16), jnp.where(lane >= DH_A, q, 0.0).astype(BF16))
    m_ref[...] = jnp.full(m_ref.shape, -jnp.inf, F32)
    l_ref[...] = jnp.zeros_like(l_ref)
    acc_ref[...] = jnp.zeros_like(acc_ref)
    krow = lax.broadcasted_iota(jnp.int32, (tq, tq), 0)
    qcol = lax.broadcasted_iota(jnp.int32, (tq, tq), 1)

    def block(kstart, diagonal):
        k = k_ref[pl.ds(kstart, tq), :]
        vt = vt_ref[:, pl.ds(kstart, tq)]
        for j in range(2):
            st = _dot_nt(k, qs[j])
            if diagonal:
                st = jnp.where(krow <= qcol, st, -jnp.inf)
            m_prev = m_ref[j]
            m_new = jnp.maximum(m_prev, jnp.max(st, axis=0, keepdims=True))
            alpha = jnp.exp(m_prev - m_new)
            p = jnp.exp(st - m_new)
            l_ref[j] = alpha * l_ref[j] + jnp.sum(p, axis=0, keepdims=True)
            acc_ref[j] = alpha * acc_ref[j] + _dot(vt, p.astype(BF16))
            m_ref[j] = m_new

    def body(ki, carry):
        block(pl.multiple_of(ki * tq, tq), False)
        return carry

    lax.fori_loop(0, qi, body, 0)
    block(pl.multiple_of(qi * tq, tq), True)

    lam = _lambda(lq1_ref, lk1_ref, lq2_ref, lk2_ref, lam_init)
    ot = acc_ref[0] / l_ref[0] - lam * (acc_ref[1] / l_ref[1])
    o_ref[...] = (_rms(ot.T, subln_ref[...]) * (1.0 - lam_init)).astype(BF16)


def _attn_prompt(q, kb, vt, lams, subln, n, t, lam_init):
    tq = 512
    nq = t // tq
    qspec = pl.BlockSpec((tq, LANES), lambda b, h, i: (b * nq + i, h))
    kspec = pl.BlockSpec((t, LANES), lambda b, h, i: (b, h))
    vtspec = pl.BlockSpec((LANES, t), lambda b, h, i: (h, b))
    small = lambda w: pl.BlockSpec((1, w), lambda b, h, i: (0, 0))
    return pl.pallas_call(
        functools.partial(_attn_kernel, tq=tq, lam_init=lam_init),
        grid=(n, H_A, nq),
        in_specs=[qspec, kspec, vtspec, small(DH_A), small(DH_A), small(DH_A), small(DH_A),
                  small(2 * DH_A)],
        out_specs=qspec,
        out_shape=jax.ShapeDtypeStruct((n * t, D_A), BF16),
        scratch_shapes=[pltpu.VMEM((2, 1, tq), F32), pltpu.VMEM((2, 1, tq), F32),
                        pltpu.VMEM((2, LANES, tq), F32)],
        compiler_params=_params(("parallel", "parallel", "arbitrary")),
        name="attn_prompt",
    )(q, kb, vt, *lams, subln)


def _attn_decode_kernel(pt_ref, q_ref, ks_ref, vs_ref, lq1_ref, lk1_ref, lq2_ref, lk2_ref, subln_ref,
                        *rest, n_pages, lam_init):
    del pt_ref
    kp_refs = rest[:n_pages]
    vp_refs = rest[n_pages:2 * n_pages]
    o_ref = rest[2 * n_pages]
    nmap = 2 * H_A
    page_rows = PAGE_SIZE * H_A
    heads = lambda x: jnp.concatenate([x[:, hh * LANES:(hh + 1) * LANES] for hh in range(H_A)], axis=0)
    q4 = heads(q_ref[0].astype(F32))
    k4 = heads(ks_ref[0].astype(BF16).astype(F32))
    v4 = heads(vs_ref[0].astype(BF16).astype(F32))
    r8 = lax.broadcasted_iota(jnp.int32, (nmap, LANES), 0)
    l8 = lax.broadcasted_iota(jnp.int32, (nmap, LANES), 1)
    q8 = jnp.where((l8 // DH_A) == (r8 // H_A), jnp.concatenate([q4, q4], axis=0), 0.0)
    q8_b = q8.astype(BF16)
    s = jnp.concatenate([_dot_nt(q8_b, kp_refs[pg][...].astype(BF16)) for pg in range(n_pages)], axis=1)
    rs = lax.broadcasted_iota(jnp.int32, s.shape, 0)
    cs = lax.broadcasted_iota(jnp.int32, s.shape, 1)
    s = jnp.where((cs % H_A) == (rs % H_A), s, -jnp.inf)
    s_self = jnp.sum(q8 * jnp.concatenate([k4, k4], axis=0), axis=1, keepdims=True)
    m = jnp.maximum(jnp.max(s, axis=1, keepdims=True), s_self)
    e = jnp.exp(s - m)
    e_self = jnp.exp(s_self - m)
    inv = 1.0 / (jnp.sum(e, axis=1, keepdims=True) + e_self)
    lam = _lambda(lq1_ref, lk1_ref, lq2_ref, lk2_ref, lam_init)
    p = e * inv
    p_self = e_self * inv
    pc = (p[0:H_A] - lam * p[H_A:nmap]).astype(BF16)
    pc_self = (p_self[0:H_A] - lam * p_self[H_A:nmap]).astype(BF16).astype(F32)
    o = pc_self * v4
    for pg in range(n_pages):
        o = o + _dot(pc[:, pg * page_rows:(pg + 1) * page_rows], vp_refs[pg][...].astype(BF16))
    o = (_rms(o, subln_ref[...]) * (1.0 - lam_init)).astype(BF16)
    for hh in range(H_A):
        o_ref[0, :, hh * LANES:(hh + 1) * LANES] = o[hh:hh + 1, :]


def _attn_decode(q, k_self, v_self, cache_k, cache_v, layer, page_table, lams, subln, lam_init):
    nb, n_pages = page_table.shape
    tok = pl.BlockSpec((1, 1, D_A), lambda b, pt: (b, 0, 0))
    small = lambda w: pl.BlockSpec((1, w), lambda b, pt: (0, 0))
    as_rows = lambda c: c.reshape(c.shape[0], c.shape[1], PAGE_SIZE * H_A, 2 * DH_A)
    cache_k, cache_v = as_rows(cache_k), as_rows(cache_v)
    page = lambda p: pl.BlockSpec((None, None, PAGE_SIZE * H_A, 2 * DH_A),
                                  lambda b, pt: (layer, pt[b, p], 0, 0))
    pages = [page(p) for p in range(n_pages)]
    grid_spec = pltpu.PrefetchScalarGridSpec(
        num_scalar_prefetch=1,
        grid=(nb,),
        in_specs=[tok, tok, tok, small(DH_A), small(DH_A), small(DH_A), small(DH_A), small(2 * DH_A)]
        + pages + pages,
        out_specs=tok,
    )
    out = pl.pallas_call(
        functools.partial(_attn_decode_kernel, n_pages=n_pages, lam_init=lam_init),
        grid_spec=grid_spec,
        out_shape=jax.ShapeDtypeStruct((nb, 1, D_A), BF16),
        compiler_params=_params(("arbitrary",)),
        name="attn_decode",
    )(page_table, q.reshape(nb, 1, D_A), k_self.reshape(nb, 1, D_A), v_self.reshape(nb, 1, D_A),
      *lams, subln, *([cache_k] * n_pages), *([cache_v] * n_pages))
    return out.reshape(nb, D_A)


def _prep_kernel(*refs, tm, tiles_per_seq):
    if tiles_per_seq:
        pb_ref, prev_ref, tail_ref = refs[:3]
        refs = refs[3:]
    else:
        pb_ref, prev_ref = refs[:2]
        refs = refs[2:]
    (mu_ref, w0_ref, a0_ref, w2_ref, a2_ref, g2_ref, kk_w_ref, ka_w_ref, rk_w_ref, ones_ref,
     kk_o, w_o, b_o, k_o, v_o, c_o, kr_o, bv_o, g_o, lw_o) = refs
    pb = pb_ref[...]
    if tiles_per_seq:
        first = (pl.program_id(0) % tiles_per_seq) == 0
        prev_row = jnp.where(first, prev_ref[0], tail_ref[SUBLANES - 1:SUBLANES, :])
        rows = lax.broadcasted_iota(jnp.int32, pb.shape, 0)
        shifted = jnp.where(rows == 0, jnp.broadcast_to(prev_row, pb.shape), pltpu.roll(pb, 1, 0))
    else:
        shifted = prev_ref[...]
    xs = pb + (shifted - pb) * mu_ref[...]
    r = xs[:, 0:D_B]
    k = xs[:, D_B:2 * D_B]
    v = xs[:, 2 * D_B:3 * D_B]
    xwa = xs[:, 3 * D_B:3 * D_B + LORA_W + LORA_A]
    xg = xs[:, 3 * D_B + LORA_W + LORA_A:]
    ones = ones_ref[...]
    w_raw = w0_ref[...] + _dot(jnp.tanh(xwa).astype(BF16), w2_ref[...])
    z = -w_raw
    softplus = jnp.maximum(z, 0.0) + jnp.log(1.0 + jnp.exp(-jnp.abs(z)))
    log_decay = -jnp.exp(-softplus - 0.5)
    decay = jnp.exp(log_decay)
    a =_sigmoid(a0_ref[...] + _dot(xwa.astype(BF16), a2_ref[...]))
    g = _dot(_sigmoid(xg).astype(BF16), g2_ref[...])
    kk = k * kk_w_ref[...]
    kk = kk / jnp.maximum(jnp.sqrt(_segsum(kk * kk, ones)), 1e-12)
    k2 = k * (1.0 + (a - 1.0) * ka_w_ref[...])
    b = kk * a
    br = _segsum(b * r, ones)
    kr = _segsum(k2 * r, ones)
    bonus = _segsum(r * k2 * rk_w_ref[...], ones)
    kk_o[...] = kk
    w_o[...] = decay
    b_o[...] = b
    k_o[...] = k2
    v_o[...] = v
    c_o[...] = decay * r - kk * br
    kr_o[...] = kr
    bv_o[...] = bonus * v
    g_o[...] = g
    lw_o[...] = log_decay


def _rwkv_prep(pb, prev, weights, tm, seq_len):
    m = pb.shape[0]
    row = lambda w: pl.BlockSpec((tm, w), lambda i: (i, 0))
    const = lambda a: pl.BlockSpec(a.shape, lambda i: (0,) * a.ndim)
    if seq_len > 1:
        tiles_per_seq = seq_len // tm
        tail = pl.BlockSpec((SUBLANES, SHIFT_DIM),
                            lambda i: (jnp.maximum(i * (tm // SUBLANES) - 1, 0), 0))
        head = [row(SHIFT_DIM), pl.BlockSpec((1, 1, SHIFT_DIM), lambda i: (i // tiles_per_seq, 0, 0)), tail]
        args = [pb, prev, pb]
    else:
        tiles_per_seq = 0
        head = [row(SHIFT_DIM), row(SHIFT_DIM)]
        args = [pb, prev]
    return pl.pallas_call(
        functools.partial(_prep_kernel, tm=tm, tiles_per_seq=tiles_per_seq),
        grid=(m // tm,),
        in_specs=head + [const(a) for a in weights],
        out_specs=[row(D_B)] * 10,
        out_shape=[jax.ShapeDtypeStruct((m, D_B), F32)] * 10,
        compiler_params=_params(("parallel",)),
        name="rwkv_prep",
    )(*args, *weights)


PAIRS = H_B // 2
CHUNK = 64
GROUP = 4 * CHUNK


def _dot3(a_hi, a_mid, b_hi, b_mid):
    return _dot(a_hi, b_hi) + _dot(a_hi, b_mid) + _dot(a_mid, b_hi)


def _scan_kernel(lw_ref, kk_ref, b_ref, k_ref, v_ref, c_ref, kr_ref, tri_ref, blk_ref,
                 y_ref, sout_ref, h_ref):
    grp = pl.program_id(1)

    @pl.when(grp == 0)
    def _():
        h_ref[...] = jnp.zeros_like(h_ref)

    n = GROUP
    tri = tri_ref[...]
    blk = blk_ref[...]
    lw = lw_ref[...]
    lw_hi = lw.astype(BF16)
    lw_r = lw - lw_hi.astype(F32)
    lw_mid = lw_r.astype(BF16)
    lw_lo = (lw_r - lw_mid.astype(F32)).astype(BF16)
    cum = _dot(tri, lw_hi) + _dot(tri, lw_mid) + _dot(tri, lw_lo)
    tot = _dot(blk, lw_hi) + _dot(blk, lw_mid) + _dot(blk, lw_lo)
    g_prev = jnp.exp(cum - lw)
    g_inv = jnp.exp(-cum)
    g_end = jnp.exp(tot - cum)
    g_tot = jnp.exp(tot)
    kk = kk_ref[...]
    b = b_ref[...]
    k = k_ref[...]
    v = v_ref[...]
    at = -(kk * g_prev)
    ct = c_ref[...] * g_prev
    bt = b * g_inv
    kt = k * g_inv
    bh = b * g_end
    kh = k * g_end
    krv = kr_ref[...] * v

    ri = lax.broadcasted_iota(jnp.int32, (n, n), 0)
    ci = lax.broadcasted_iota(jnp.int32, (n, n), 1)
    first = (ri // CHUNK) * CHUNK
    stril = ((ci - first) | (ri - 1 - ci)) >= 0
    eye_n = (ri == ci).astype(F32)
    lane = lax.broadcasted_iota(jnp.int32, (n, LANES), 1)
    rown = lax.broadcasted_iota(jnp.int32, (n, LANES), 0)
    head0 = lane < DH_B
    r2 = lax.broadcasted_iota(jnp.int32, (LANES, LANES), 0)
    c2 = lax.broadcasted_iota(jnp.int32, (LANES, LANES), 1)
    same_head = (r2 // DH_B) == (c2 // DH_B)
    eye_l = r2 == c2

    for p in range(PAIRS):
        sl = slice(p * LANES, (p + 1) * LANES)
        at_p, ct_p, v_p = at[:, sl], ct[:, sl], v[:, sl]
        v_b = v_p.astype(BF16)
        bk = jnp.concatenate([bt[:, sl], kt[:, sl]], axis=0).astype(BF16)
        per_head = []
        for j in range(2):
            mine = head0 if j == 0 else jnp.logical_not(head0)
            lhs = jnp.concatenate([jnp.where(mine, at_p, 0.0), jnp.where(mine, ct_p, 0.0)], axis=0)
            x = _dot_nt(lhs.astype(BF16), bk)
            lab = jnp.where(stril, x[:n, :n], 0.0)
            lak = jnp.where(stril, x[:n, n:], 0.0)
            mcb = jnp.where(stril, x[n:, :n], 0.0)
            mck = jnp.where(stril, x[n:, n:], 0.0)
            tm = eye_n + lab
            xp = lab
            for _ in range(5):
                xb = xp.astype(BF16)
                xp = _dot(xb, xb)
                tm = tm + _dot(tm.astype(BF16), xp.astype(BF16))
            gm = _dot(lak.astype(BF16), v_b)
            tag = _dot(tm.astype(BF16), jnp.concatenate([at_p, gm], axis=1).astype(BF16))
            mt = _dot(mcb.astype(BF16), tag.astype(BF16))
            mv = _dot(mck.astype(BF16), v_b)
            per_head.append((tag[:, :LANES], tag[:, LANES:], mt[:, :LANES], mt[:, LANES:] + mv))
        pick = lambda i: jnp.where(head0, per_head[0][i], per_head[1][i])
        ta, tg = pick(0), pick(1)
        cy = (ct_p + pick(2)).astype(BF16)
        yg = pick(3) + krv[:, sl]
        bht_hi, bht_mid = _split2(bh[:, sl].T)
        kht_hi, kht_mid = _split2(kh[:, sl].T)
        h = h_ref[p]
        ys = []
        for cidx in range(GROUP // CHUNK):
            rows = slice(cidx * CHUNK, (cidx + 1) * CHUNK)
            in_chunk = (rown // CHUNK) == cidx
            ta_hi, ta_mid = _split2(jnp.where(in_chunk, ta, 0.0))
            tg_hi, tg_mid = _split2(jnp.where(in_chunk, tg, 0.0))
            vc_hi, vc_mid = _split2(jnp.where(in_chunk, v_p, 0.0))
            decay_c = jnp.broadcast_to(g_tot[cidx * CHUNK:cidx * CHUNK + 1, sl], (LANES, LANES))
            pm = jnp.where(eye_l, decay_c, 0.0) + jnp.where(same_head, _dot3(bht_hi, bht_mid, ta_hi, ta_mid), 0.0)
            qm = jnp.where(same_head, _dot3(bht_hi, bht_mid, tg_hi, tg_mid)
                           + _dot3(kht_hi, kht_mid, vc_hi, vc_mid), 0.0)
            ys.append(_dot(cy[rows], h.astype(BF16)) + yg[rows])
            h = _dot3(*_split2(pm), *_split2(h)) + qm
        h_ref[p] = h
        y_ref[:, sl] = jnp.concatenate(ys, axis=0)

    @pl.when(grp == pl.num_programs(1) - 1)
    def _():
        for p in range(PAIRS):
            st = h_ref[p].T
            sout_ref[0, 2 * p] = st[:DH_B, :DH_B]
            sout_ref[0, 2 * p + 1] = st[DH_B:, DH_B:]


def _rwkv_scan(vecs, n, t):
    idx = jnp.arange(GROUP)
    same = (idx[:, None] // CHUNK) == (idx[None, :] // CHUNK)
    tri = (same & (idx[None, :] <= idx[:, None])).astype(BF16)
    blk_ones = same.astype(BF16)
    groups = t // GROUP
    blk = pl.BlockSpec((GROUP, D_B), lambda s, g: (s * groups + g, 0))
    const = pl.BlockSpec((GROUP, GROUP), lambda s, g: (0, 0))
    return pl.pallas_call(
        _scan_kernel,
        grid=(n, groups),
        in_specs=[blk] * 7 + [const, const],
        out_specs=[blk, pl.BlockSpec((1, H_B, DH_B, DH_B), lambda s, g: (s, 0, 0, 0))],
        out_shape=[jax.ShapeDtypeStruct((n * t, D_B), F32),
                   jax.ShapeDtypeStruct((n, H_B, DH_B, DH_B), F32)],
        scratch_shapes=[pltpu.VMEM((PAIRS, LANES, LANES), F32)],
        compiler_params=_params(("parallel", "arbitrary")),
        name="rwkv_scan",
    )(*vecs, tri, blk_ones)


def _wkv_step_kernel(s_ref, kk_ref, w_ref, b_ref, k_ref, v_ref, c_ref, kr_ref, y_ref, so_ref, *, nh):
    sub = lax.broadcasted_iota(jnp.int32, (DH_B, DH_B), 0)
    lan = lax.broadcasted_iota(jnp.int32, (DH_B, DH_B), 1)
    diag = (sub == lan).astype(F32)

    def body(i, carry):
        s = s_ref[i]
        v = v_ref[i]
        sa = -jnp.sum(s * kk_ref[i], axis=1, keepdims=True)
        z = jnp.sum(s * c_ref[i], axis=1, keepdims=True)
        vcol = jnp.sum(diag * v, axis=1, keepdims=True)
        so_ref[i] = s * w_ref[i] + sa * b_ref[i] + vcol * k_ref[i]
        y_ref[i] = jnp.sum(diag * z, axis=0, keepdims=True) + v * kr_ref[i]
        return carry

    lax.fori_loop(0, nh, body, 0, unroll=8)


def _wkv_step(state, vecs):
    nb = state.shape[0]
    nh_total = nb * H_B
    nh = 64
    sblk = pl.BlockSpec((nh, DH_B, DH_B), lambda i: (i, 0, 0))
    vblk = pl.BlockSpec((nh, 1, DH_B), lambda i: (i, 0, 0))
    y, s_new = pl.pallas_call(
        functools.partial(_wkv_step_kernel, nh=nh),
        grid=(nh_total // nh,),
        in_specs=[sblk] + [vblk] * 7,
        out_specs=[vblk, sblk],
        out_shape=[jax.ShapeDtypeStruct((nh_total, 1, DH_B), F32),
                   jax.ShapeDtypeStruct((nh_total, DH_B, DH_B), F32)],
        compiler_params=_params(("parallel",)),
        name="wkv_step",
    )(state.reshape(nh_total, DH_B, DH_B), *[x.reshape(nh_total, 1, DH_B) for x in vecs])
    return y.reshape(nb, D_B), s_new.reshape(nb, H_B, DH_B, DH_B)


def _outproj_kernel(ya_ref, y_ref, bv_ref, g_ref, lnw_ref, lnb_ref, ones_ref, h_ref, wo_ref, gpost_ref,
                    o_ref):
    ones = ones_ref[...]
    y = y_ref[...]
    mean = _segsum(y, ones) * (1.0 / DH_B)
    d = y - mean
    var = _segsum(d * d, ones) * (1.0 / DH_B)
    yn = d * lax.rsqrt(var + GN_EPS) * lnw_ref[...] + lnb_ref[...]
    yb = ((yn + bv_ref[...]) * g_ref[...]).astype(BF16)
    mix = _dot(ya_ref[...], wo_ref[0:D_A, :]) + _dot(yb, wo_ref[D_A:, :])
    o_ref[...] = h_ref[...] + _rms(mix, gpost_ref[...])


def _outproj(ya, y, bv, g, ln_w, ln_b, ones, h, w_out, g_post, tm):
    m = h.shape[0]
    row = lambda w: pl.BlockSpec((tm, w), lambda i: (i, 0))
    const = lambda a: pl.BlockSpec(a.shape, lambda i: (0,) * a.ndim)
    return pl.pallas_call(
        _outproj_kernel,
        grid=(m // tm,),
        in_specs=[row(D_A), row(D_B), row(D_B), row(D_B), const(ln_w), const(ln_b), const(ones),
                  row(D_MODEL), const(w_out), const(g_post)],
        out_specs=row(D_MODEL),
        out_shape=jax.ShapeDtypeStruct((m, D_MODEL), F32),
        compiler_params=_params(("parallel",)),
        name="outproj",
    )(ya, y, bv, g, ln_w, ln_b, ones, h, w_out, g_post)


def _block_ones(n, seg):
    i = jnp.arange(n) // seg
    return (i[:, None] == i[None, :]).astype(BF16)


def kernel(x_prompt, x_sample, cache_k, cache_v, state_wkv, state_shift, page_table, n_ffn1_pre, n_ffn1_post, ffn1_gate, ffn1_up, ffn1_down, n_mix_pre, n_mix_post, w_in, w_out, lambda_q1, lambda_k1, lambda_q2, lambda_k2, subln, mu_shift, w0, w2, a0, a2, g2, k_k, k_a, r_k, ln_x_w, ln_x_b, n_ffn2_pre, n_ffn2_post, ffn2_gate, ffn2_up, ffn2_down):
    n_p, t_p, _ = x_prompt.shape
    n_s, t_s, _ = x_sample.shape
    assert t_s == 1
    depth = w_in.shape[0]
    n_pages = page_table.shape[1]
    past_len = n_pages * PAGE_SIZE
    ones_seg = _block_ones(D_B, DH_B)
    tab_p = _rope_tables(jnp.arange(t_p, dtype=jnp.int32))
    tab_s = _rope_tables(jnp.full((n_s,), past_len, jnp.int32))
    zeros_lora = jnp.zeros((LORA_W, D_B), F32)
    tm_p = 512
    tm_proj = 512

    yp = x_prompt.reshape(n_p * t_p, D_MODEL)
    ys = x_sample.reshape(n_s, D_MODEL)
    outs = [[] for _ in range(8)]
    for l in range(depth):
        lam_init = 0.8 - 0.6 * math.exp(-0.3 * l)
        vec = lambda a: a[l].reshape(1, -1)
        ffn1 = (vec(n_ffn1_pre), vec(n_ffn1_post), ffn1_gate[l].astype(BF16), ffn1_up[l].astype(BF16),
                ffn1_down[l].astype(BF16))
        ffn2 = (vec(n_ffn2_pre), vec(n_ffn2_post), ffn2_gate[l].astype(BF16), ffn2_up[l].astype(BF16),
                ffn2_down[l].astype(BF16))
        w_in_b = w_in[l].astype(BF16)
        w_out_b = w_out[l].astype(BF16)
        lams = (vec(lambda_q1), vec(lambda_k1), vec(lambda_q2), vec(lambda_k2))
        prep_w = (vec(mu_shift), vec(w0), vec(a0),
                  jnp.concatenate([w2[l], zeros_lora], axis=0).astype(BF16),
                  jnp.concatenate([zeros_lora, a2[l]], axis=0).astype(BF16),
                  g2[l].astype(BF16), vec(k_k), vec(k_a), r_k[l].reshape(1, D_B), ones_seg)

        def mix_tail(h, ya, y, bv, g, tm):
            h2 = _outproj(ya, y, bv, g, vec(ln_x_w), vec(ln_x_b), ones_seg, h, w_out_b, vec(n_mix_post), tm)
            return _ffn(h2, *ffn2, tm)

        h = _ffn(yp, *ffn1, tm_p)
        q, k, v, kb, vt, pb = _proj(h, vec(n_mix_pre), w_in_b, tab_p, tm_proj, t_p // tm_proj)
        ya = _attn_prompt(q, kb, vt, lams, vec(subln), n_p, t_p, lam_init)
        prev0 = jnp.zeros((n_p, 1, SHIFT_DIM), F32)
        kk_, _, b_, k2_, v_, c_, kr_, bv_, g_, lw_ = _rwkv_prep(pb, prev0, prep_w, tm_proj, t_p)
        y, s_new = _rwkv_scan((lw_, kk_, b_, k2_, v_, c_, kr_), n_p, t_p)
        yp = mix_tail(h, ya, y, bv_, g_, tm_p)
        outs[0].append(k.reshape(n_p, t_p, H_A, 2 * DH_A))
        outs[1].append(v.reshape(n_p, t_p, H_A, 2 * DH_A))
        outs[2].append(s_new)
        outs[3].append(pb.reshape(n_p, t_p, SHIFT_DIM)[:, -1])

        h = _ffn(ys, *ffn1, n_s)
        q, k, v, _, _, pb = _proj(h, vec(n_mix_pre), w_in_b, tab_s, n_s, 1)
        ya = _attn_decode(q, k, v, cache_k, cache_v, l, page_table, lams, vec(subln), lam_init)
        kk_, w_, b_, k2_, v_, c_, kr_, bv_, g_, _ = _rwkv_prep(pb, state_shift[l], prep_w, n_s, 1)
        y, s_new = _wkv_step(state_wkv[l], (kk_, w_, b_, k2_, v_, c_, kr_))
        ys = mix_tail(h, ya, y, bv_, g_, n_s)
        outs[4].append(k.reshape(n_s, 1, H_A, 2 * DH_A))
        outs[5].append(v.reshape(n_s, 1, H_A, 2 * DH_A))
        outs[6].append(s_new)
        outs[7].append(pb)

    return (yp.reshape(n_p, t_p, D_MODEL), ys.reshape(n_s, 1, D_MODEL),
            *[jnp.stack(o) for o in outs])
```

```python
import functools
import math

import jax
import jax.numpy as jnp
from jax import lax
from jax.experimental import pallas as pl
from jax.experimental.pallas import tpu as pltpu

F32 = jnp.float32
BF16 = jnp.bfloat16

D_MODEL = 1024
H_A = 4
DH_A = 64
D_A = H_A * 2 * DH_A
ROT_DIM = DH_A // 4
ROPE_THETA = 500000.0
H_B = 8
DH_B = 64
D_B = H_B * DH_B
LORA_W = 64
LORA_A = 64
LORA_G = 128
SHIFT_DIM = 3 * D_B + LORA_W + LORA_A + LORA_G
D_IN = 3 * D_A + SHIFT_DIM
D_FF = 2816
PAGE_SIZE = 128
NORM_EPS = 1e-6
GN_EPS = 64e-5

LANES = 128
SUBLANES = 8
VMEM_LIMIT = 48 * 1024 * 1024


def _dot(a, b):
    return jnp.dot(a, b, preferred_element_type=F32)


def _dot_nt(a, b):
    return lax.dot_general(a, b, (((1,), (1,)), ((), ())), preferred_element_type=F32)


def _rms(x, g):
    return x * lax.rsqrt(jnp.mean(x * x, axis=-1, keepdims=True) + NORM_EPS) * g


def _sigmoid(x):
    return 1.0 / (1.0 + jnp.exp(-x))


def _split2(x):
    hi = x.astype(BF16)
    mid = (x - hi.astype(F32)).astype(BF16)
    return hi, mid


def _segsum(x, ones):
    hi = x.astype(BF16)
    r1 = x - hi.astype(F32)
    mid = r1.astype(BF16)
    lo = (r1 - mid.astype(F32)).astype(BF16)
    return _dot(hi, ones) + _dot(mid, ones) + _dot(lo, ones)


def _params(sem):
    return pltpu.CompilerParams(dimension_semantics=sem, vmem_limit_bytes=VMEM_LIMIT)


def _ffn_kernel(x_ref, gpre_ref, gpost_ref, wg_ref, wu_ref, wd_ref, o_ref):
    x = x_ref[...]
    un = _rms(x, gpre_ref[...]).astype(BF16)
    g = _dot(un, wg_ref[...])
    u = _dot(un, wu_ref[...])
    hid = ((g * _sigmoid(g)) * u).astype(BF16)
    o_ref[...] = x + 0.5 * _rms(_dot(hid, wd_ref[...]), gpost_ref[...])


def _ffn(x, g_pre, g_post, wg, wu, wd, tm):
    m = x.shape[0]
    row = pl.BlockSpec((tm, D_MODEL), lambda i: (i, 0))
    vec = pl.BlockSpec((1, D_MODEL), lambda i: (0, 0))
    resident = lambda w: pl.BlockSpec(w.shape, lambda i: (0, 0), pipeline_mode=pl.Buffered(1))
    return pl.pallas_call(
        _ffn_kernel,
        grid=(m // tm,),
        in_specs=[row, vec, vec, resident(wg), resident(wu), resident(wd)],
        out_specs=row,
        out_shape=jax.ShapeDtypeStruct((m, D_MODEL), F32),
        compiler_params=_params(("parallel",)),
        name="ffn",
    )(x, g_pre, g_post, wg, wu, wd)


def _proj_kernel(h_ref, g_ref, w_ref, cos_ref, sina_ref, sinb_ref,
                 q_ref, k_ref, v_ref, kb_ref, vt_ref, pb_ref):
    u = _rms(h_ref[...], g_ref[...]).astype(BF16)
    cos = cos_ref[...]
    sina = sina_ref[...]
    sinb = sinb_ref[...]
    half = ROT_DIM // 2

    def rope(x):
        return x * cos + pltpu.roll(x, LANES - half, 1) * sina + pltpu.roll(x, half, 1) * sinb

    qa = _dot(u, w_ref[:, 0:D_A])
    ka = _dot(u, w_ref[:, D_A:2 * D_A])
    for hh in range(H_A):
        sl = slice(hh * LANES, (hh + 1) * LANES)
        q_ref[:, sl] = (rope(qa[:, sl]) * (DH_A ** -0.5)).astype(BF16)
        kh = rope(ka[:, sl])
        k_ref[:, sl] = kh
        kb_ref[:, sl] = kh.astype(BF16)
    va = _dot(u, w_ref[:, 2 * D_A:3 * D_A])
    v_ref[...] = va
    vt_ref[...] = va.T.astype(BF16)
    pb_ref[...] = _dot(u, w_ref[:, 3 * D_A:])


def _proj(h, g, w_in, tables, tm, table_blocks):
    m = h.shape[0]
    row = lambda width: pl.BlockSpec((tm, width), lambda i: (i, 0))
    tab = pl.BlockSpec((tm, LANES), lambda i: (i % table_blocks, 0))
    shp = lambda width, dt: jax.ShapeDtypeStruct((m, width), dt)
    return pl.pallas_call(
        _proj_kernel,
        grid=(m // tm,),
        in_specs=[row(D_MODEL), pl.BlockSpec((1, D_MODEL), lambda i: (0, 0)),
                  pl.BlockSpec((D_MODEL, D_IN), lambda i: (0, 0)), tab, tab, tab],
        out_specs=[row(D_A), row(D_A), row(D_A), row(D_A), pl.BlockSpec((D_A, tm), lambda i: (0, i)),
                   row(SHIFT_DIM)],
        out_shape=[shp(D_A, BF16), shp(D_A, F32), shp(D_A, F32), shp(D_A, BF16),
                   jax.ShapeDtypeStruct((D_A, m), BF16), shp(SHIFT_DIM, F32)],
        compiler_params=_params(("parallel",)),
        name="proj",
    )(h, g, w_in, *tables)


def _rope_tables(pos):
    half = ROT_DIM // 2
    t = pos.shape[0]
    inv_freq = ROPE_THETA ** (-jnp.arange(half, dtype=F32) / half)
    ang = pos.astype(F32)[:, None] * inv_freq[None, :]
    cos = jnp.cos(ang)
    sin = jnp.sin(ang)
    rest = DH_A - ROT_DIM
    cos64 = jnp.concatenate([cos, cos, jnp.ones((t, rest), F32)], axis=1)
    sina64 = jnp.concatenate([-sin, jnp.zeros((t, half + rest), F32)], axis=1)
    sinb64 = jnp.concatenate([jnp.zeros((t, half), F32), sin, jnp.zeros((t, rest), F32)], axis=1)
    two = lambda x: jnp.concatenate([x, x], axis=1)
    return two(cos64), two(sina64), two(sinb64)


def _lambda(lq1_ref, lk1_ref, lq2_ref, lk2_ref, lam_init):
    s1 = jnp.sum(lq1_ref[...] * lk1_ref[...], axis=-1, keepdims=True)
    s2 = jnp.sum(lq2_ref[...] * lk2_ref[...], axis=-1, keepdims=True)
    return jnp.exp(s1) - jnp.exp(s2) + lam_init


def _attn_kernel(q_ref, k_ref, vt_ref, lq1_ref, lk1_ref, lq2_ref, lk2_ref, subln_ref, o_ref,
                 m_ref, acc_ref, *, tq, lam_init):
    qi = pl.program_id(2)
    q = q_ref[...].astype(F32)
    lane = lax.broadcasted_iota(jnp.int32, q.shape, 1)
    qs = (jnp.where(lane < DH_A, q, 0.0).astype(BF16), jnp.where(lane >= DH_A, q, 0.0).astype(BF16))
    m_ref[...] = jnp.full(m_ref.shape, -jnp.inf, F32)
    acc_ref[...] = jnp.zeros_like(acc_ref)
    krow = lax.broadcasted_iota(jnp.int32, (tq, tq), 0)
    qcol = lax.broadcasted_iota(jnp.int32, (tq, tq), 1)
    ones_rows = jnp.ones((acc_ref.shape[1] - LANES, tq), BF16)

    def block(kstart, diagonal):
        k = k_ref[pl.ds(kstart, tq), :]
        vt = jnp.concatenate([vt_ref[:, pl.ds(kstart, tq)], ones_rows], axis=0)
        maps = range(2)
        st = [_dot_nt(k, qs[j]) for j in maps]
        if diagonal:
            st = [jnp.where(krow <= qcol, s, -jnp.inf) for s in st]
        m_prev = [m_ref[j] for j in maps]
        m_new = [jnp.maximum(m_prev[j], jnp.max(st[j], axis=0, keepdims=True)) for j in maps]
        p = [jnp.exp(st[j] - m_new[j]) for j in maps]
        alpha = [jnp.exp(m_prev[j] - m_new[j]) for j in maps]
        pv = [_dot(vt, p[j].astype(BF16)) for j in maps]
        for j in maps:
            acc_ref[j] = alpha[j] * acc_ref[j] + pv[j]
            m_ref[j] = m_new[j]

    def body(ki, carry):
        block(pl.multiple_of(ki * tq, tq), False)
        return carry

    lax.fori_loop(0, qi, body, 0)
    block(pl.multiple_of(qi * tq, tq), True)

    lam = _lambda(lq1_ref, lk1_ref, lq2_ref, lk2_ref, lam_init)
    norm = lambda j: acc_ref[j, 0:LANES, :] / acc_ref[j, LANES:LANES + 1, :]
    ot = norm(0) - lam * norm(1)
    o_ref[...] = (_rms(ot.T, subln_ref[...]) * (1.0 - lam_init)).astype(BF16)


def _attn_prompt(q, kb, vt, lams, subln, n, t, lam_init):
    tq = 512
    nq = t // tq
    qspec = pl.BlockSpec((tq, LANES), lambda b, h, i: (b * nq + i, h))
    kspec = pl.BlockSpec((t, LANES), lambda b, h, i: (b, h))
    vtspec = pl.BlockSpec((LANES, t), lambda b, h, i: (h, b))
    small = lambda w: pl.BlockSpec((1, w), lambda b, h, i: (0, 0))
    return pl.pallas_call(
        functools.partial(_attn_kernel, tq=tq, lam_init=lam_init),
        grid=(n, H_A, nq),
        in_specs=[qspec, kspec, vtspec, small(DH_A), small(DH_A), small(DH_A), small(DH_A),
                  small(2 * DH_A)],
        out_specs=qspec,
        out_shape=jax.ShapeDtypeStruct((n * t, D_A), BF16),
        scratch_shapes=[pltpu.VMEM((2, 1, tq), F32), pltpu.VMEM((2, LANES + 16, tq), F32)],
        compiler_params=_params(("parallel", "parallel", "arbitrary")),
        name="attn_prompt",
    )(q, kb, vt, *lams, subln)


def _attn_decode_kernel(pt_ref, q_ref, ks_ref, vs_ref, lq1_ref, lk1_ref, lq2_ref, lk2_ref, subln_ref,
                        *rest, n_pages, lam_init):
    del pt_ref
    kp_refs = rest[:n_pages]
    vp_refs = rest[n_pages:2 * n_pages]
    o_ref = rest[2 * n_pages]
    nmap = 2 * H_A
    page_rows = PAGE_SIZE * H_A
    heads = lambda x: jnp.concatenate([x[:, hh * LANES:(hh + 1) * LANES] for hh in range(H_A)], axis=0)
    q4 = heads(q_ref[0].astype(F32))
    k4 = heads(ks_ref[0].astype(BF16).astype(F32))
    v4 = heads(vs_ref[0].astype(BF16).astype(F32))
    r8 = lax.broadcasted_iota(jnp.int32, (nmap, LANES), 0)
    l8 = lax.broadcasted_iota(jnp.int32, (nmap, LANES), 1)
    q8 = jnp.where((l8 // DH_A) == (r8 // H_A), jnp.concatenate([q4, q4], axis=0), 0.0)
    q8_b = q8.astype(BF16)
    s = jnp.concatenate([_dot_nt(q8_b, kp_refs[pg][...].astype(BF16)) for pg in range(n_pages)], axis=1)
    rs = lax.broadcasted_iota(jnp.int32, s.shape, 0)
    cs = lax.broadcasted_iota(jnp.int32, s.shape, 1)
    s = jnp.where((cs % H_A) == (rs % H_A), s, -jnp.inf)
    s_self = jnp.sum(q8 * jnp.concatenate([k4, k4], axis=0), axis=1, keepdims=True)
    m = jnp.maximum(jnp.max(s, axis=1, keepdims=True), s_self)
    e = jnp.exp(s - m)
    e_self = jnp.exp(s_self - m)
    inv = 1.0 / (jnp.sum(e, axis=1, keepdims=True) + e_self)
    lam = _lambda(lq1_ref, lk1_ref, lq2_ref, lk2_ref, lam_init)
    p = e * inv
    p_self = e_self * inv
    pc = (p[0:H_A] - lam * p[H_A:nmap]).astype(BF16)
    pc_self = (p_self[0:H_A] - lam * p_self[H_A:nmap]).astype(BF16).astype(F32)
    o = pc_self * v4
    for pg in range(n_pages):
        o = o + _dot(pc[:, pg * page_rows:(pg + 1) * page_rows], vp_refs[pg][...].astype(BF16))
    o = (_rms(o, subln_ref[...]) * (1.0 - lam_init)).astype(BF16)
    for hh in range(H_A):
        o_ref[0, :, hh * LANES:(hh + 1) * LANES] = o[hh:hh + 1, :]


def _attn_decode(q, k_self, v_self, cache_k, cache_v, layer, page_table, lams, subln, lam_init):
    nb, n_pages = page_table.shape
    tok = pl.BlockSpec((1, 1, D_A), lambda b, pt: (b, 0, 0))
    small = lambda w: pl.BlockSpec((1, w), lambda b, pt: (0, 0))
    as_rows = lambda c: c.reshape(c.shape[0], c.shape[1], PAGE_SIZE * H_A, 2 * DH_A)
    cache_k, cache_v = as_rows(cache_k), as_rows(cache_v)
    page = lambda p: pl.BlockSpec((None, None, PAGE_SIZE * H_A, 2 * DH_A),
                                  lambda b, pt: (layer, pt[b, p], 0, 0))
    pages = [page(p) for p in range(n_pages)]
    grid_spec = pltpu.PrefetchScalarGridSpec(
        num_scalar_prefetch=1,
        grid=(nb,),
        in_specs=[tok, tok, tok, small(DH_A), small(DH_A), small(DH_A), small(DH_A), small(2 * DH_A)]
        + pages + pages,
        out_specs=tok,
    )
    out = pl.pallas_call(
        functools.partial(_attn_decode_kernel, n_pages=n_pages, lam_init=lam_init),
        grid_spec=grid_spec,
        out_shape=jax.ShapeDtypeStruct((nb, 1, D_A), BF16),
        compiler_params=_params(("arbitrary",)),
        name="attn_decode",
    )(page_table, q.reshape(nb, 1, D_A), k_self.reshape(nb, 1, D_A), v_self.reshape(nb, 1, D_A),
      *lams, subln, *([cache_k] * n_pages), *([cache_v] * n_pages))
    return out.reshape(nb, D_A)


def _prep_kernel(*refs, tm, tiles_per_seq):
    if tiles_per_seq:
        pb_ref, prev_ref, tail_ref = refs[:3]
        refs = refs[3:]
    else:
        pb_ref, prev_ref = refs[:2]
        refs = refs[2:]
    (mu_ref, w0_ref, a0_ref, w2_ref, a2_ref, g2_ref, kk_w_ref, ka_w_ref, rk_w_ref, ones_ref,
     kk_o, w_o, b_o, k_o, v_o, c_o, kr_o, bv_o, g_o, lw_o) = refs
    pb = pb_ref[...]
    if tiles_per_seq:
        first = (pl.program_id(0) % tiles_per_seq) == 0
        prev_row = jnp.where(first, prev_ref[0], tail_ref[SUBLANES - 1:SUBLANES, :])
        rows = lax.broadcasted_iota(jnp.int32, pb.shape, 0)
        shifted = jnp.where(rows == 0, jnp.broadcast_to(prev_row, pb.shape), pltpu.roll(pb, 1, 0))
    else:
        shifted = prev_ref[...]
    xs = pb + (shifted - pb) * mu_ref[...]
    r = xs[:, 0:D_B]
    k = xs[:, D_B:2 * D_B]
    v = xs[:, 2 * D_B:3 * D_B]
    xwa = xs[:, 3 * D_B:3 * D_B + LORA_W + LORA_A]
    xg = xs[:, 3 * D_B + LORA_W + LORA_A:]
    ones = ones_ref[...]
    w_raw = w0_ref[...] + _dot(jnp.tanh(xwa).astype(BF16), w2_ref[...])
    z = -w_raw
    softplus = jnp.maximum(z, 0.0) + jnp.log(1.0 + jnp.exp(-jnp.abs(z)))
    log_decay = -jnp.exp(-softplus - 0.5)
    decay = jnp.exp(log_decay)
    a =_sigmoid(a0_ref[...] + _dot(xwa.astype(BF16), a2_ref[...]))
    g = _dot(_sigmoid(xg).astype(BF16), g2_ref[...])
    kk = k * kk_w_ref[...]
    kk = kk / jnp.maximum(jnp.sqrt(_segsum(kk * kk, ones)), 1e-12)
    k2 = k * (1.0 + (a - 1.0) * ka_w_ref[...])
    b = kk * a
    br = _segsum(b * r, ones)
    kr = _segsum(k2 * r, ones)
    bonus = _segsum(r * k2 * rk_w_ref[...], ones)
    kk_o[...] = kk
    w_o[...] = decay
    b_o[...] = b
    k_o[...] = k2
    v_o[...] = v
    c_o[...] = decay * r - kk * br
    kr_o[...] = kr
    bv_o[...] = bonus * v
    g_o[...] = g
    lw_o[...] = log_decay


def _rwkv_prep(pb, prev, weights, tm, seq_len):
    m = pb.shape[0]
    row = lambda w: pl.BlockSpec((tm, w), lambda i: (i, 0))
    const = lambda a: pl.BlockSpec(a.shape, lambda i: (0,) * a.ndim)
    if seq_len > 1:
        tiles_per_seq = seq_len // tm
        tail = pl.BlockSpec((SUBLANES, SHIFT_DIM),
                            lambda i: (jnp.maximum(i * (tm // SUBLANES) - 1, 0), 0))
        head = [row(SHIFT_DIM), pl.BlockSpec((1, 1, SHIFT_DIM), lambda i: (i // tiles_per_seq, 0, 0)), tail]
        args = [pb, prev, pb]
    else:
        tiles_per_seq = 0
        head = [row(SHIFT_DIM), row(SHIFT_DIM)]
        args = [pb, prev]
    return pl.pallas_call(
        functools.partial(_prep_kernel, tm=tm, tiles_per_seq=tiles_per_seq),
        grid=(m // tm,),
        in_specs=head + [const(a) for a in weights],
        out_specs=[row(D_B)] * 10,
        out_shape=[jax.ShapeDtypeStruct((m, D_B), F32)] * 10,
        compiler_params=_params(("parallel",)),
        name="rwkv_prep",
    )(*args, *weights)


PAIRS = H_B // 2
CHUNK = 64
GROUP = 4 * CHUNK


def _dot3(a_hi, a_mid, b_hi, b_mid):
    return _dot(a_hi, b_hi) + _dot(a_hi, b_mid) + _dot(a_mid, b_hi)


def _scan_kernel(lw_ref, kk_ref, b_ref, k_ref, v_ref, c_ref, kr_ref, tri_ref, blk_ref,
                 y_ref, sout_ref, h_ref):
    grp = pl.program_id(1)

    @pl.when(grp == 0)
    def _():
        h_ref[...] = jnp.zeros_like(h_ref)

    n = GROUP
    tri = tri_ref[...]
    blk = blk_ref[...]
    lw = lw_ref[...]
    lw_hi = lw.astype(BF16)
    lw_r = lw - lw_hi.astype(F32)
    lw_mid = lw_r.astype(BF16)
    lw_lo = (lw_r - lw_mid.astype(F32)).astype(BF16)
    cum = _dot(tri, lw_hi) + _dot(tri, lw_mid) + _dot(tri, lw_lo)
    tot = _dot(blk, lw_hi) + _dot(blk, lw_mid) + _dot(blk, lw_lo)
    g_prev = jnp.exp(cum - lw)
    g_inv = jnp.exp(-cum)
    g_end = jnp.exp(tot - cum)
    g_tot = jnp.exp(tot)
    kk = kk_ref[...]
    b = b_ref[...]
    k = k_ref[...]
    v = v_ref[...]
    at = -(kk * g_prev)
    ct = c_ref[...] * g_prev
    bt = b * g_inv
    kt = k * g_inv
    bh = b * g_end
    kh = k * g_end
    krv = kr_ref[...] * v

    ri = lax.broadcasted_iota(jnp.int32, (n, n), 0)
    ci = lax.broadcasted_iota(jnp.int32, (n, n), 1)
    first = (ri // CHUNK) * CHUNK
    stril = ((ci - first) | (ri - 1 - ci)) >= 0
    eye_n = (ri == ci).astype(F32)
    lane = lax.broadcasted_iota(jnp.int32, (n, LANES), 1)
    rown = lax.broadcasted_iota(jnp.int32, (n, LANES), 0)
    head0 = lane < DH_B
    r2 = lax.broadcasted_iota(jnp.int32, (LANES, LANES), 0)
    c2 = lax.broadcasted_iota(jnp.int32, (LANES, LANES), 1)
    same_head = (r2 // DH_B) == (c2 // DH_B)
    eye_l = r2 == c2

    heads = [(p, j) for p in range(PAIRS) for j in range(2)]
    sls = [slice(p * LANES, (p + 1) * LANES) for p in range(PAIRS)]
    bf = lambda x: x.astype(BF16)
    v_b = [bf(v[:, sl]) for sl in sls]
    bk = [bf(jnp.concatenate([bt[:, sl], kt[:, sl]], axis=0)) for sl in sls]
    xs = []
    for p, j in heads:
        mine = head0 if j == 0 else lane >= DH_B
        lhs = jnp.concatenate([jnp.where(mine, at[:, sls[p]], 0.0), jnp.where(mine, ct[:, sls[p]], 0.0)], axis=0)
        xs.append(_dot_nt(bf(lhs), bk[p]))
    lab = [jnp.where(stril, x[:n, :n], 0.0) for x in xs]
    lak = [jnp.where(stril, x[:n, n:], 0.0) for x in xs]
    mcb = [jnp.where(stril, x[n:, :n], 0.0) for x in xs]
    mck = [jnp.where(stril, x[n:, n:], 0.0) for x in xs]
    tm = [eye_n + x for x in lab]
    xp = lab
    for _ in range(5):
        xb = [bf(x) for x in xp]
        xp = [_dot(x, x) for x in xb]
        tm = [t + _dot(bf(t), bf(x)) for t, x in zip(tm, xp)]
    gm = [_dot(bf(lak[i]), v_b[p]) for i, (p, j) in enumerate(heads)]
    tag = [_dot(bf(tm[i]), bf(jnp.concatenate([at[:, sls[p]], gm[i]], axis=1)))
           for i, (p, j) in enumerate(heads)]
    mt = [_dot(bf(mcb[i]), bf(tag[i])) for i in range(len(heads))]
    mv = [_dot(bf(mck[i]), v_b[p]) for i, (p, j) in enumerate(heads)]

    ta, tg, cy, yg, bht, kht, hs = [], [], [], [], [], [], []
    for p in range(PAIRS):
        pick = lambda f: jnp.where(head0, f(2 * p), f(2 * p + 1))
        ta.append(pick(lambda i: tag[i][:, :LANES]))
        tg.append(pick(lambda i: tag[i][:, LANES:]))
        cy.append(bf(ct[:, sls[p]] + pick(lambda i: mt[i][:, :LANES])))
        yg.append(pick(lambda i: mt[i][:, LANES:] + mv[i]) + krv[:, sls[p]])
        bht.append(bf(bh[:, sls[p]].T))
        kht.append(bf(kh[:, sls[p]].T))
        hs.append(h_ref[p])
    ys = [[] for _ in range(PAIRS)]
    for cidx in range(GROUP // CHUNK):
        rows = slice(cidx * CHUNK, (cidx + 1) * CHUNK)
        in_chunk = (rown // CHUNK) == cidx
        for p in range(PAIRS):
            only = lambda x: bf(jnp.where(in_chunk, x, 0.0))
            decay_c = jnp.broadcast_to(g_tot[cidx * CHUNK:cidx * CHUNK + 1, sls[p]], (LANES, LANES))
            pm = jnp.where(eye_l, decay_c, 0.0) + jnp.where(same_head, _dot(bht[p], only(ta[p])), 0.0)
            qm = jnp.where(same_head, _dot(bht[p], only(tg[p])) + _dot(kht[p], only(v[:, sls[p]])), 0.0)
            ys[p].append(_dot(cy[p][rows], bf(hs[p])) + yg[p][rows])
            hs[p] = _dot3(*_split2(pm), *_split2(hs[p])) + qm
    for p in range(PAIRS):
        h_ref[p] = hs[p]
        y_ref[:, sls[p]] = jnp.concatenate(ys[p], axis=0)

    @pl.when(grp == pl.num_programs(1) - 1)
    def _():
        for p in range(PAIRS):
            st = h_ref[p].T
            sout_ref[0, 2 * p] = st[:DH_B, :DH_B]
            sout_ref[0, 2 * p + 1] = st[DH_B:, DH_B:]


def _rwkv_scan(vecs, n, t):
    idx = jnp.arange(GROUP)
    same = (idx[:, None] // CHUNK) == (idx[None, :] // CHUNK)
    tri = (same & (idx[None, :] <= idx[:, None])).astype(BF16)
    blk_ones = same.astype(BF16)
    groups = t // GROUP
    blk = pl.BlockSpec((GROUP, D_B), lambda s, g: (s * groups + g, 0))
    const = pl.BlockSpec((GROUP, GROUP), lambda s, g: (0, 0))
    return pl.pallas_call(
        _scan_kernel,
        grid=(n, groups),
        in_specs=[blk] * 7 + [const, const],
        out_specs=[blk, pl.BlockSpec((1, H_B, DH_B, DH_B), lambda s, g: (s, 0, 0, 0))],
        out_shape=[jax.ShapeDtypeStruct((n * t, D_B), F32),
                   jax.ShapeDtypeStruct((n, H_B, DH_B, DH_B), F32)],
        scratch_shapes=[pltpu.VMEM((PAIRS, LANES, LANES), F32)],
        compiler_params=_params(("parallel", "arbitrary")),
        name="rwkv_scan",
    )(*vecs, tri, blk_ones)


def _wkv_step_kernel(s_ref, kk_ref, w_ref, b_ref, k_ref, v_ref, c_ref, kr_ref, y_ref, so_ref, *, nh):
    sub = lax.broadcasted_iota(jnp.int32, (DH_B, DH_B), 0)
    lan = lax.broadcasted_iota(jnp.int32, (DH_B, DH_B), 1)
    diag = (sub == lan).astype(F32)

    def body(i, carry):
        s = s_ref[i]
        v = v_ref[i]
        sa = -jnp.sum(s * kk_ref[i], axis=1, keepdims=True)
        z = jnp.sum(s * c_ref[i], axis=1, keepdims=True)
        vcol = jnp.sum(diag * v, axis=1, keepdims=True)
        so_ref[i] = s * w_ref[i] + sa * b_ref[i] + vcol * k_ref[i]
        y_ref[i] = jnp.sum(diag * z, axis=0, keepdims=True) + v * kr_ref[i]
        return carry

    lax.fori_loop(0, nh, body, 0, unroll=8)


def _wkv_step(state, vecs):
    nb = state.shape[0]
    nh_total = nb * H_B
    nh = 64
    sblk = pl.BlockSpec((nh, DH_B, DH_B), lambda i: (i, 0, 0))
    vblk = pl.BlockSpec((nh, 1, DH_B), lambda i: (i, 0, 0))
    y, s_new = pl.pallas_call(
        functools.partial(_wkv_step_kernel, nh=nh),
        grid=(nh_total // nh,),
        in_specs=[sblk] + [vblk] * 7,
        out_specs=[vblk, sblk],
        out_shape=[jax.ShapeDtypeStruct((nh_total, 1, DH_B), F32),
                   jax.ShapeDtypeStruct((nh_total, DH_B, DH_B), F32)],
        compiler_params=_params(("parallel",)),
        name="wkv_step",
    )(state.reshape(nh_total, DH_B, DH_B), *[x.reshape(nh_total, 1, DH_B) for x in vecs])
    return y.reshape(nb, D_B), s_new.reshape(nb, H_B, DH_B, DH_B)


def _outproj_kernel(ya_ref, y_ref, bv_ref, g_ref, lnw_ref, lnb_ref, ones_ref, h_ref, wo_ref, gpost_ref,
                    o_ref):
    ones = ones_ref[...]
    y = y_ref[...]
    mean = _segsum(y, ones) * (1.0 / DH_B)
    d = y - mean
    var = _segsum(d * d, ones) * (1.0 / DH_B)
    yn = d * lax.rsqrt(var + GN_EPS) * lnw_ref[...] + lnb_ref[...]
    yb = ((yn + bv_ref[...]) * g_ref[...]).astype(BF16)
    mix = _dot(ya_ref[...], wo_ref[0:D_A, :]) + _dot(yb, wo_ref[D_A:, :])
    o_ref[...] = h_ref[...] + _rms(mix, gpost_ref[...])


def _outproj(ya, y, bv, g, ln_w, ln_b, ones, h, w_out, g_post, tm):
    m = h.shape[0]
    row = lambda w: pl.BlockSpec((tm, w), lambda i: (i, 0))
    const = lambda a: pl.BlockSpec(a.shape, lambda i: (0,) * a.ndim)
    return pl.pallas_call(
        _outproj_kernel,
        grid=(m // tm,),
        in_specs=[row(D_A), row(D_B), row(D_B), row(D_B), const(ln_w), const(ln_b), const(ones),
                  row(D_MODEL), const(w_out), const(g_post)],
        out_specs=row(D_MODEL),
        out_shape=jax.ShapeDtypeStruct((m, D_MODEL), F32),
        compiler_params=_params(("parallel",)),
        name="outproj",
    )(ya, y, bv, g, ln_w, ln_b, ones, h, w_out, g_post)


def _block_ones(n, seg):
    i = jnp.arange(n) // seg
    return (i[:, None] == i[None, :]).astype(BF16)


def kernel(x_prompt, x_sample, cache_k, cache_v, state_wkv, state_shift, page_table, n_ffn1_pre, n_ffn1_post, ffn1_gate, ffn1_up, ffn1_down, n_mix_pre, n_mix_post, w_in, w_out, lambda_q1, lambda_k1, lambda_q2, lambda_k2, subln, mu_shift, w0, w2, a0, a2, g2, k_k, k_a, r_k, ln_x_w, ln_x_b, n_ffn2_pre, n_ffn2_post, ffn2_gate, ffn2_up, ffn2_down):
    n_p, t_p, _ = x_prompt.shape
    n_s, t_s, _ = x_sample.shape
    assert t_s == 1
    depth = w_in.shape[0]
    n_pages = page_table.shape[1]
    past_len = n_pages * PAGE_SIZE
    ones_seg = _block_ones(D_B, DH_B)
    tab_p = _rope_tables(jnp.arange(t_p, dtype=jnp.int32))
    tab_s = _rope_tables(jnp.full((n_s,), past_len, jnp.int32))
    zeros_lora = jnp.zeros((LORA_W, D_B), F32)
    tm_p = 512
    tm_proj = 512

    yp = x_prompt.reshape(n_p * t_p, D_MODEL)
    ys = x_sample.reshape(n_s, D_MODEL)
    outs = [[] for _ in range(8)]
    for l in range(depth):
        lam_init = 0.8 - 0.6 * math.exp(-0.3 * l)
        vec = lambda a: a[l].reshape(1, -1)
        ffn1 = (vec(n_ffn1_pre), vec(n_ffn1_post), ffn1_gate[l].astype(BF16), ffn1_up[l].astype(BF16),
                ffn1_down[l].astype(BF16))
        ffn2 = (vec(n_ffn2_pre), vec(n_ffn2_post), ffn2_gate[l].astype(BF16), ffn2_up[l].astype(BF16),
                ffn2_down[l].astype(BF16))
        w_in_b = w_in[l].astype(BF16)
        w_out_b = w_out[l].astype(BF16)
        lams = (vec(lambda_q1), vec(lambda_k1), vec(lambda_q2), vec(lambda_k2))
        prep_w = (vec(mu_shift), vec(w0), vec(a0),
                  jnp.concatenate([w2[l], zeros_lora], axis=0).astype(BF16),
                  jnp.concatenate([zeros_lora, a2[l]], axis=0).astype(BF16),
                  g2[l].astype(BF16), vec(k_k), vec(k_a), r_k[l].reshape(1, D_B), ones_seg)

        def mix_tail(h, ya, y, bv, g, tm):
            h2 = _outproj(ya, y, bv, g, vec(ln_x_w), vec(ln_x_b), ones_seg, h, w_out_b, vec(n_mix_post), tm)
            return _ffn(h2, *ffn2, tm)

        h = _ffn(yp, *ffn1, tm_p)
        q, k, v, kb, vt, pb = _proj(h, vec(n_mix_pre), w_in_b, tab_p, tm_proj, t_p // tm_proj)
        ya = _attn_prompt(q, kb, vt, lams, vec(subln), n_p, t_p, lam_init)
        prev0 = jnp.zeros((n_p, 1, SHIFT_DIM), F32)
        kk_, _, b_, k2_, v_, c_, kr_, bv_, g_, lw_ = _rwkv_prep(pb, prev0, prep_w, tm_proj, t_p)
        y, s_new = _rwkv_scan((lw_, kk_, b_, k2_, v_, c_, kr_), n_p, t_p)
        yp = mix_tail(h, ya, y, bv_, g_, tm_p)
        outs[0].append(k.reshape(n_p, t_p, H_A, 2 * DH_A))
        outs[1].append(v.reshape(n_p, t_p, H_A, 2 * DH_A))
        outs[2].append(s_new)
        outs[3].append(pb.reshape(n_p, t_p, SHIFT_DIM)[:, -1])

        h = _ffn(ys, *ffn1, n_s)
        q, k, v, _, _, pb = _proj(h, vec(n_mix_pre), w_in_b, tab_s, n_s, 1)
        ya = _attn_decode(q, k, v, cache_k, cache_v, l, page_table, lams, vec(subln), lam_init)
        kk_, w_, b_, k2_, v_, c_, kr_, bv_, g_, _ = _rwkv_prep(pb, state_shift[l], prep_w, n_s, 1)
        y, s_new = _wkv_step(state_wkv[l], (kk_, w_, b_, k2_, v_, c_, kr_))
        ys = mix_tail(h, ya, y, bv_, g_, n_s)
        outs[4].append(k.reshape(n_s, 1, H_A, 2 * DH_A))
        outs[5].append(v.reshape(n_s, 1, H_A, 2 * DH_A))
        outs[6].append(s_new)
        outs[7].append(pb)

    return (yp.reshape(n_p, t_p, D_MODEL), ys.reshape(n_s, 1, D_MODEL),
            *[jnp.stack(o) for o in outs])
```

```python
import functools
import math

import jax
import jax.numpy as jnp
from jax import lax
from jax.experimental import pallas as pl
from jax.experimental.pallas import tpu as pltpu

F32 = jnp.float32
BF16 = jnp.bfloat16

D_MODEL = 1024
H_A = 4
DH_A = 64
D_A = H_A * 2 * DH_A
ROT_DIM = DH_A // 4
ROPE_THETA = 500000.0
H_B = 8
DH_B = 64
D_B = H_B * DH_B
LORA_W = 64
LORA_A = 64
LORA_G = 128
SHIFT_DIM = 3 * D_B + LORA_W + LORA_A + LORA_G
D_IN = 3 * D_A + SHIFT_DIM
D_FF = 2816
PAGE_SIZE = 128
NORM_EPS = 1e-6
GN_EPS = 64e-5

LANES = 128
SUBLANES = 8
VMEM_LIMIT = 48 * 1024 * 1024


def _dot(a, b):
    return jnp.dot(a, b, preferred_element_type=F32)


def _dot_nt(a, b):
    return lax.dot_general(a, b, (((1,), (1,)), ((), ())), preferred_element_type=F32)


def _rms(x, g):
    return x * lax.rsqrt(jnp.mean(x * x, axis=-1, keepdims=True) + NORM_EPS) * g


def _sigmoid(x):
    return 1.0 / (1.0 + jnp.exp(-x))


def _split2(x):
    hi = x.astype(BF16)
    mid = (x - hi.astype(F32)).astype(BF16)
    return hi, mid


def _segsum(x, ones):
    hi = x.astype(BF16)
    r1 = x - hi.astype(F32)
    mid = r1.astype(BF16)
    lo = (r1 - mid.astype(F32)).astype(BF16)
    return _dot(hi, ones) + _dot(mid, ones) + _dot(lo, ones)


def _params(sem):
    return pltpu.CompilerParams(dimension_semantics=sem, vmem_limit_bytes=VMEM_LIMIT)


def _ffn_half_step(x, gpre_ref, gpost_ref, wg_ref, wu_ref, wd_ref):
    un = _rms(x, gpre_ref[...]).astype(BF16)
    g = _dot(un, wg_ref[...])
    u = _dot(un, wu_ref[...])
    hid = ((g * _sigmoid(g)) * u).astype(BF16)
    return x + 0.5 * _rms(_dot(hid, wd_ref[...]), gpost_ref[...])


def _ffn_kernel(x_ref, gpre_ref, gpost_ref, wg_ref, wu_ref, wd_ref, o_ref):
    o_ref[...] = _ffn_half_step(x_ref[...], gpre_ref, gpost_ref, wg_ref, wu_ref, wd_ref)


def _ffn(x, g_pre, g_post, wg, wu, wd, tm):
    m = x.shape[0]
    row = pl.BlockSpec((tm, D_MODEL), lambda i: (i, 0))
    vec = pl.BlockSpec((1, D_MODEL), lambda i: (0, 0))
    resident = lambda w: pl.BlockSpec(w.shape, lambda i: (0, 0), pipeline_mode=pl.Buffered(1))
    return pl.pallas_call(
        _ffn_kernel,
        grid=(m // tm,),
        in_specs=[row, vec, vec, resident(wg), resident(wu), resident(wd)],
        out_specs=row,
        out_shape=jax.ShapeDtypeStruct((m, D_MODEL), F32),
        compiler_params=_params(("parallel",)),
        name="ffn",
    )(x, g_pre, g_post, wg, wu, wd)


def _proj_kernel(h_ref, g_ref, w_ref, cos_ref, sina_ref, sinb_ref,
                 q_ref, k_ref, v_ref, kb_ref, vt_ref, pb_ref):
    u = _rms(h_ref[...], g_ref[...]).astype(BF16)
    cos = cos_ref[...]
    sina = sina_ref[...]
    sinb = sinb_ref[...]
    half = ROT_DIM // 2

    def rope(x):
        return x * cos + pltpu.roll(x, LANES - half, 1) * sina + pltpu.roll(x, half, 1) * sinb

    qa = _dot(u, w_ref[:, 0:D_A])
    ka = _dot(u, w_ref[:, D_A:2 * D_A])
    for hh in range(H_A):
        sl = slice(hh * LANES, (hh + 1) * LANES)
        q_ref[:, sl] = (rope(qa[:, sl]) * (DH_A ** -0.5)).astype(BF16)
        kh = rope(ka[:, sl])
        k_ref[:, hh, :] = kh
        kb_ref[:, sl] = kh.astype(BF16)
    va = _dot(u, w_ref[:, 2 * D_A:3 * D_A])
    for hh in range(H_A):
        v_ref[:, hh, :] = va[:, hh * LANES:(hh + 1) * LANES]
    vt_ref[...] = va.T.astype(BF16)
    pb_ref[...] = _dot(u, w_ref[:, 3 * D_A:])


def _proj(h, g, w_in, tables, tm, table_blocks):
    m = h.shape[0]
    row = lambda width: pl.BlockSpec((tm, width), lambda i: (i, 0))
    tab = pl.BlockSpec((tm, LANES), lambda i: (i % table_blocks, 0))
    per_head = pl.BlockSpec((tm, H_A, 2 * DH_A), lambda i: (i, 0, 0))
    shp = lambda width, dt: jax.ShapeDtypeStruct((m, width), dt)
    return pl.pallas_call(
        _proj_kernel,
        grid=(m // tm,),
        in_specs=[row(D_MODEL), pl.BlockSpec((1, D_MODEL), lambda i: (0, 0)),
                  pl.BlockSpec((D_MODEL, D_IN), lambda i: (0, 0)), tab, tab, tab],
        out_specs=[row(D_A), per_head, per_head, row(D_A), pl.BlockSpec((D_A, tm), lambda i: (0, i)),
                   row(SHIFT_DIM)],
        out_shape=[shp(D_A, BF16), jax.ShapeDtypeStruct((m, H_A, 2 * DH_A), F32),
                   jax.ShapeDtypeStruct((m, H_A, 2 * DH_A), F32), shp(D_A, BF16),
                   jax.ShapeDtypeStruct((D_A, m), BF16), shp(SHIFT_DIM, F32)],
        compiler_params=_params(("parallel",)),
        name="proj",
    )(h, g, w_in, *tables)


def _rope_tables(pos):
    half = ROT_DIM // 2
    t = pos.shape[0]
    inv_freq = ROPE_THETA ** (-jnp.arange(half, dtype=F32) / half)
    ang = pos.astype(F32)[:, None] * inv_freq[None, :]
    cos = jnp.cos(ang)
    sin = jnp.sin(ang)
    rest = DH_A - ROT_DIM
    cos64 = jnp.concatenate([cos, cos, jnp.ones((t, rest), F32)], axis=1)
    sina64 = jnp.concatenate([-sin, jnp.zeros((t, half + rest), F32)], axis=1)
    sinb64 = jnp.concatenate([jnp.zeros((t, half), F32), sin, jnp.zeros((t, rest), F32)], axis=1)
    two = lambda x: jnp.concatenate([x, x], axis=1)
    return two(cos64), two(sina64), two(sinb64)


def _lambda(lq1_ref, lk1_ref, lq2_ref, lk2_ref, lam_init):
    s1 = jnp.sum(lq1_ref[...] * lk1_ref[...], axis=-1, keepdims=True)
    s2 = jnp.sum(lq2_ref[...] * lk2_ref[...], axis=-1, keepdims=True)
    return jnp.exp(s1) - jnp.exp(s2) + lam_init


def _attn_kernel(q_ref, k_ref, vt_ref, lq1_ref, lk1_ref, lq2_ref, lk2_ref, subln_ref, o_ref,
                 m_ref, acc_ref, sta_ref, stb_ref, *, tq, lam_init):
    qi = pl.program_id(2)
    q = q_ref[...].astype(F32)
    lane = lax.broadcasted_iota(jnp.int32, q.shape, 1)
    qs = (jnp.where(lane < DH_A, q, 0.0).astype(BF16), jnp.where(lane >= DH_A, q, 0.0).astype(BF16))
    m_ref[...] = jnp.full(m_ref.shape, -jnp.inf, F32)
    acc_ref[...] = jnp.zeros_like(acc_ref)
    krow = lax.broadcasted_iota(jnp.int32, (tq, tq), 0)
    qcol = lax.broadcasted_iota(jnp.int32, (tq, tq), 1)
    ones_rows = jnp.ones((acc_ref.shape[1] - LANES, tq), BF16)

    maps = range(2)

    def score(kstart, st_ref):
        k = k_ref[pl.ds(kstart, tq), :]
        for j in maps:
            st_ref[j] = _dot_nt(k, qs[j])

    def consume(kstart, st_ref, diagonal):
        vt = jnp.concatenate([vt_ref[:, pl.ds(kstart, tq)], ones_rows], axis=0)
        st = [st_ref[j] for j in maps]
        if diagonal:
            st = [jnp.where(krow <= qcol, s, -jnp.inf) for s in st]
        m_prev = [m_ref[j] for j in maps]
        m_new = [jnp.maximum(m_prev[j], jnp.max(st[j], axis=0, keepdims=True)) for j in maps]
        p = [jnp.exp(st[j] - m_new[j]) for j in maps]
        alpha = [jnp.exp(m_prev[j] - m_new[j]) for j in maps]
        pv = [_dot(vt, p[j].astype(BF16)) for j in maps]
        for j in maps:
            acc_ref[j] = alpha[j] * acc_ref[j] + pv[j]
            m_ref[j] = m_new[j]

    start = lambda i: pl.multiple_of(i * tq, tq)
    score(0, sta_ref)

    def body(ki, carry):
        for parity, (cur, nxt) in enumerate(((sta_ref, stb_ref), (stb_ref, sta_ref))):
            @pl.when(ki % 2 == parity)
            def _(cur=cur, nxt=nxt):
                score(start(ki + 1), nxt)
                consume(start(ki), cur, False)
        return carry

    lax.fori_loop(0, qi, body, 0)
    for parity, cur in enumerate((sta_ref, stb_ref)):
        @pl.when(qi % 2 == parity)
        def _(cur=cur):
            consume(start(qi), cur, True)

    lam = _lambda(lq1_ref, lk1_ref, lq2_ref, lk2_ref, lam_init)
    norm = lambda j: acc_ref[j, 0:LANES, :] / acc_ref[j, LANES:LANES + 1, :]
    ot = norm(0) - lam * norm(1)
    o_ref[...] = (_rms(ot.T, subln_ref[...]) * (1.0 - lam_init)).astype(BF16)


def _attn_prompt(q, kb, vt, lams, subln, n, t, lam_init):
    tq = 512
    nq = t // tq
    qspec = pl.BlockSpec((tq, LANES), lambda b, h, i: (b * nq + i, h))
    kspec = pl.BlockSpec((t, LANES), lambda b, h, i: (b, h))
    vtspec = pl.BlockSpec((LANES, t), lambda b, h, i: (h, b))
    small = lambda w: pl.BlockSpec((1, w), lambda b, h, i: (0, 0))
    return pl.pallas_call(
        functools.partial(_attn_kernel, tq=tq, lam_init=lam_init),
        grid=(n, H_A, nq),
        in_specs=[qspec, kspec, vtspec, small(DH_A), small(DH_A), small(DH_A), small(DH_A),
                  small(2 * DH_A)],
        out_specs=qspec,
        out_shape=jax.ShapeDtypeStruct((n * t, D_A), BF16),
        scratch_shapes=[pltpu.VMEM((2, 1, tq), F32), pltpu.VMEM((2, LANES + 16, tq), F32),
                        pltpu.VMEM((2, tq, tq), F32), pltpu.VMEM((2, tq, tq), F32)],
        compiler_params=_params(("parallel", "parallel", "arbitrary")),
        name="attn_prompt",
    )(q, kb, vt, *lams, subln)


def _attn_decode_kernel(pt_ref, q_ref, ks_ref, vs_ref, lq1_ref, lk1_ref, lq2_ref, lk2_ref, subln_ref,
                        *rest, n_pages, lam_init):
    del pt_ref
    kp_refs = rest[:n_pages]
    vp_refs = rest[n_pages:2 * n_pages]
    o_ref = rest[2 * n_pages]
    nmap = 2 * H_A
    page_rows = PAGE_SIZE * H_A
    heads = lambda x: jnp.concatenate([x[:, hh * LANES:(hh + 1) * LANES] for hh in range(H_A)], axis=0)
    q4 = heads(q_ref[0].astype(F32))
    k4 = heads(ks_ref[0].astype(BF16).astype(F32))
    v4 = heads(vs_ref[0].astype(BF16).astype(F32))
    r8 = lax.broadcasted_iota(jnp.int32, (nmap, LANES), 0)
    l8 = lax.broadcasted_iota(jnp.int32, (nmap, LANES), 1)
    q8 = jnp.where((l8 // DH_A) == (r8 // H_A), jnp.concatenate([q4, q4], axis=0), 0.0)
    q8_b = q8.astype(BF16)
    s = jnp.concatenate([_dot_nt(q8_b, kp_refs[pg][...].astype(BF16)) for pg in range(n_pages)], axis=1)
    rs = lax.broadcasted_iota(jnp.int32, s.shape, 0)
    cs = lax.broadcasted_iota(jnp.int32, s.shape, 1)
    s = jnp.where((cs % H_A) == (rs % H_A), s, -jnp.inf)
    s_self = jnp.sum(q8 * jnp.concatenate([k4, k4], axis=0), axis=1, keepdims=True)
    m = jnp.maximum(jnp.max(s, axis=1, keepdims=True), s_self)
    e = jnp.exp(s - m)
    e_self = jnp.exp(s_self - m)
    inv = 1.0 / (jnp.sum(e, axis=1, keepdims=True) + e_self)
    lam = _lambda(lq1_ref, lk1_ref, lq2_ref, lk2_ref, lam_init)
    p = e * inv
    p_self = e_self * inv
    pc = (p[0:H_A] - lam * p[H_A:nmap]).astype(BF16)
    pc_self = (p_self[0:H_A] - lam * p_self[H_A:nmap]).astype(BF16).astype(F32)
    o = pc_self * v4
    for pg in range(n_pages):
        o = o + _dot(pc[:, pg * page_rows:(pg + 1) * page_rows], vp_refs[pg][...].astype(BF16))
    o = (_rms(o, subln_ref[...]) * (1.0 - lam_init)).astype(BF16)
    for hh in range(H_A):
        o_ref[0, :, hh * LANES:(hh + 1) * LANES] = o[hh:hh + 1, :]


def _attn_decode(q, k_self, v_self, cache_k, cache_v, layer, page_table, lams, subln, lam_init):
    nb, n_pages = page_table.shape
    tok = pl.BlockSpec((1, 1, D_A), lambda b, pt: (b, 0, 0))
    small = lambda w: pl.BlockSpec((1, w), lambda b, pt: (0, 0))
    as_rows = lambda c: c.reshape(c.shape[0], c.shape[1], PAGE_SIZE * H_A, 2 * DH_A)
    cache_k, cache_v = as_rows(cache_k), as_rows(cache_v)
    page = lambda p: pl.BlockSpec((None, None, PAGE_SIZE * H_A, 2 * DH_A),
                                  lambda b, pt: (layer, pt[b, p], 0, 0))
    pages = [page(p) for p in range(n_pages)]
    grid_spec = pltpu.PrefetchScalarGridSpec(
        num_scalar_prefetch=1,
        grid=(nb,),
        in_specs=[tok, tok, tok, small(DH_A), small(DH_A), small(DH_A), small(DH_A), small(2 * DH_A)]
        + pages + pages,
        out_specs=tok,
    )
    out = pl.pallas_call(
        functools.partial(_attn_decode_kernel, n_pages=n_pages, lam_init=lam_init),
        grid_spec=grid_spec,
        out_shape=jax.ShapeDtypeStruct((nb, 1, D_A), BF16),
        compiler_params=_params(("arbitrary",)),
        name="attn_decode",
    )(page_table, q.reshape(nb, 1, D_A), k_self.reshape(nb, 1, D_A), v_self.reshape(nb, 1, D_A),
      *lams, subln, *([cache_k] * n_pages), *([cache_v] * n_pages))
    return out.reshape(nb, D_A)


def _prep_kernel(*refs, tm, tiles_per_seq):
    if tiles_per_seq:
        pb_ref, prev_ref, tail_ref = refs[:3]
        refs = refs[3:]
    else:
        pb_ref, prev_ref = refs[:2]
        refs = refs[2:]
    (mu_ref, w0_ref, a0_ref, w2_ref, a2_ref, g2_ref, kk_w_ref, ka_w_ref, rk_w_ref, ones_ref,
     kk_o, w_o, b_o, k_o, v_o, c_o, kr_o, bv_o, g_o, lw_o) = refs
    pb = pb_ref[...]
    if tiles_per_seq:
        first = (pl.program_id(0) % tiles_per_seq) == 0
        prev_row = jnp.where(first, prev_ref[0], tail_ref[SUBLANES - 1:SUBLANES, :])
        rows = lax.broadcasted_iota(jnp.int32, pb.shape, 0)
        shifted = jnp.where(rows == 0, jnp.broadcast_to(prev_row, pb.shape), pltpu.roll(pb, 1, 0))
    else:
        shifted = prev_ref[...]
    xs = pb + (shifted - pb) * mu_ref[...]
    r = xs[:, 0:D_B]
    k = xs[:, D_B:2 * D_B]
    v = xs[:, 2 * D_B:3 * D_B]
    xwa = xs[:, 3 * D_B:3 * D_B + LORA_W + LORA_A]
    xg = xs[:, 3 * D_B + LORA_W + LORA_A:]
    ones = ones_ref[...]
    w_raw = w0_ref[...] + _dot(jnp.tanh(xwa).astype(BF16), w2_ref[...])
    z = -w_raw
    softplus = jnp.maximum(z, 0.0) + jnp.log(1.0 + jnp.exp(-jnp.abs(z)))
    log_decay = -jnp.exp(-softplus - 0.5)
    decay = jnp.exp(log_decay)
    a =_sigmoid(a0_ref[...] + _dot(xwa.astype(BF16), a2_ref[...]))
    g = _dot(_sigmoid(xg).astype(BF16), g2_ref[...])
    kk = k * kk_w_ref[...]
    kk = kk / jnp.maximum(jnp.sqrt(_segsum(kk * kk, ones)), 1e-12)
    k2 = k * (1.0 + (a - 1.0) * ka_w_ref[...])
    b = kk * a
    br = _segsum(b * r, ones)
    kr = _segsum(k2 * r, ones)
    bonus = _segsum(r * k2 * rk_w_ref[...], ones)
    kk_o[...] = kk
    w_o[...] = decay
    b_o[...] = b
    k_o[...] = k2
    v_o[...] = v
    c_o[...] = decay * r - kk * br
    kr_o[...] = kr
    bv_o[...] = bonus * v
    g_o[...] = g
    lw_o[...] = log_decay


def _rwkv_prep(pb, prev, weights, tm, seq_len):
    m = pb.shape[0]
    row = lambda w: pl.BlockSpec((tm, w), lambda i: (i, 0))
    const = lambda a: pl.BlockSpec(a.shape, lambda i: (0,) * a.ndim)
    if seq_len > 1:
        tiles_per_seq = seq_len // tm
        tail = pl.BlockSpec((SUBLANES, SHIFT_DIM),
                            lambda i: (jnp.maximum(i * (tm // SUBLANES) - 1, 0), 0))
        head = [row(SHIFT_DIM), pl.BlockSpec((1, 1, SHIFT_DIM), lambda i: (i // tiles_per_seq, 0, 0)), tail]
        args = [pb, prev, pb]
    else:
        tiles_per_seq = 0
        head = [row(SHIFT_DIM), row(SHIFT_DIM)]
        args = [pb, prev]
    return pl.pallas_call(
        functools.partial(_prep_kernel, tm=tm, tiles_per_seq=tiles_per_seq),
        grid=(m // tm,),
        in_specs=head + [const(a) for a in weights],
        out_specs=[row(D_B)] * 10,
        out_shape=[jax.ShapeDtypeStruct((m, D_B), F32)] * 10,
        compiler_params=_params(("parallel",)),
        name="rwkv_prep",
    )(*args, *weights)


PAIRS = H_B // 2
CHUNK = 64
GROUP = 4 * CHUNK


def _dot3(a_hi, a_mid, b_hi, b_mid):
    return _dot(a_hi, b_hi) + _dot(a_hi, b_mid) + _dot(a_mid, b_hi)


def _scan_kernel(lw_ref, kk_ref, b_ref, k_ref, v_ref, c_ref, kr_ref, tri_ref, blk_ref,
                 y_ref, sout_ref, h_ref):
    grp = pl.program_id(1)

    @pl.when(grp == 0)
    def _():
        h_ref[...] = jnp.zeros_like(h_ref)

    n = GROUP
    tri = tri_ref[...]
    blk = blk_ref[...]
    lw = lw_ref[...]
    lw_hi = lw.astype(BF16)
    lw_r = lw - lw_hi.astype(F32)
    lw_mid = lw_r.astype(BF16)
    lw_lo = (lw_r - lw_mid.astype(F32)).astype(BF16)
    cum = _dot(tri, lw_hi) + _dot(tri, lw_mid) + _dot(tri, lw_lo)
    tot = _dot(blk, lw_hi) + _dot(blk, lw_mid) + _dot(blk, lw_lo)
    g_prev = jnp.exp(cum - lw)
    g_inv = jnp.exp(-cum)
    g_end = jnp.exp(tot - cum)
    g_tot = jnp.exp(tot)
    kk = kk_ref[...]
    b = b_ref[...]
    k = k_ref[...]
    v = v_ref[...]
    at = -(kk * g_prev)
    ct = c_ref[...] * g_prev
    bt = b * g_inv
    kt = k * g_inv
    bh = b * g_end
    kh = k * g_end
    krv = kr_ref[...] * v

    ri = lax.broadcasted_iota(jnp.int32, (n, n), 0)
    ci = lax.broadcasted_iota(jnp.int32, (n, n), 1)
    first = (ri // CHUNK) * CHUNK
    stril = ((ci - first) | (ri - 1 - ci)) >= 0
    eye_n = (ri == ci).astype(F32)
    lane = lax.broadcasted_iota(jnp.int32, (n, LANES), 1)
    rown = lax.broadcasted_iota(jnp.int32, (n, LANES), 0)
    head0 = lane < DH_B
    r2 = lax.broadcasted_iota(jnp.int32, (LANES, LANES), 0)
    c2 = lax.broadcasted_iota(jnp.int32, (LANES, LANES), 1)
    same_head = (r2 // DH_B) == (c2 // DH_B)
    eye_l = r2 == c2

    heads = [(p, j) for p in range(PAIRS) for j in range(2)]
    sls = [slice(p * LANES, (p + 1) * LANES) for p in range(PAIRS)]
    bf = lambda x: x.astype(BF16)
    v_b = [bf(v[:, sl]) for sl in sls]
    bk = [bf(jnp.concatenate([bt[:, sl], kt[:, sl]], axis=0)) for sl in sls]
    xs = []
    for p, j in heads:
        mine = head0 if j == 0 else lane >= DH_B
        lhs = jnp.concatenate([jnp.where(mine, at[:, sls[p]], 0.0), jnp.where(mine, ct[:, sls[p]], 0.0)], axis=0)
        xs.append(_dot_nt(bf(lhs), bk[p]))
    lab = [jnp.where(stril, x[:n, :n], 0.0) for x in xs]
    lak = [jnp.where(stril, x[:n, n:], 0.0) for x in xs]
    mcb = [jnp.where(stril, x[n:, :n], 0.0) for x in xs]
    mck = [jnp.where(stril, x[n:, n:], 0.0) for x in xs]
    tm = [eye_n + x for x in lab]
    xp = lab
    for _ in range(5):
        xb = [bf(x) for x in xp]
        xp = [_dot(x, x) for x in xb]
        tm = [t + _dot(bf(t), bf(x)) for t, x in zip(tm, xp)]
    gm = [_dot(bf(lak[i]), v_b[p]) for i, (p, j) in enumerate(heads)]
    tag = [_dot(bf(tm[i]), bf(jnp.concatenate([at[:, sls[p]], gm[i]], axis=1)))
           for i, (p, j) in enumerate(heads)]
    mt = [_dot(bf(mcb[i]), bf(tag[i])) for i in range(len(heads))]
    mv = [_dot(bf(mck[i]), v_b[p]) for i, (p, j) in enumerate(heads)]

    ta, tg, cy, yg, bht, kht, hs = [], [], [], [], [], [], []
    for p in range(PAIRS):
        pick = lambda f: jnp.where(head0, f(2 * p), f(2 * p + 1))
        ta.append(pick(lambda i: tag[i][:, :LANES]))
        tg.append(pick(lambda i: tag[i][:, LANES:]))
        cy.append(bf(ct[:, sls[p]] + pick(lambda i: mt[i][:, :LANES])))
        yg.append(pick(lambda i: mt[i][:, LANES:] + mv[i]) + krv[:, sls[p]])
        bht.append(bf(bh[:, sls[p]].T))
        kht.append(bf(kh[:, sls[p]].T))
        hs.append(h_ref[p])
    ys = [[] for _ in range(PAIRS)]
    for cidx in range(GROUP // CHUNK):
        rows = slice(cidx * CHUNK, (cidx + 1) * CHUNK)
        in_chunk = (rown // CHUNK) == cidx
        for p in range(PAIRS):
            only = lambda x: bf(jnp.where(in_chunk, x, 0.0))
            decay_c = jnp.broadcast_to(g_tot[cidx * CHUNK:cidx * CHUNK + 1, sls[p]], (LANES, LANES))
            pm = jnp.where(eye_l, decay_c, 0.0) + jnp.where(same_head, _dot(bht[p], only(ta[p])), 0.0)
            qm = jnp.where(same_head, _dot(bht[p], only(tg[p])) + _dot(kht[p], only(v[:, sls[p]])), 0.0)
            ys[p].append(_dot(cy[p][rows], bf(hs[p])) + yg[p][rows])
            hs[p] = _dot3(*_split2(pm), *_split2(hs[p])) + qm
    for p in range(PAIRS):
        h_ref[p] = hs[p]
        y_ref[:, sls[p]] = jnp.concatenate(ys[p], axis=0)

    @pl.when(grp == pl.num_programs(1) - 1)
    def _():
        for p in range(PAIRS):
            st = h_ref[p].T
            sout_ref[0, 2 * p] = st[:DH_B, :DH_B]
            sout_ref[0, 2 * p + 1] = st[DH_B:, DH_B:]


def _rwkv_scan(vecs, n, t):
    idx = jnp.arange(GROUP)
    same = (idx[:, None] // CHUNK) == (idx[None, :] // CHUNK)
    tri = (same & (idx[None, :] <= idx[:, None])).astype(BF16)
    blk_ones = same.astype(BF16)
    groups = t // GROUP
    blk = pl.BlockSpec((GROUP, D_B), lambda s, g: (s * groups + g, 0))
    const = pl.BlockSpec((GROUP, GROUP), lambda s, g: (0, 0))
    return pl.pallas_call(
        _scan_kernel,
        grid=(n, groups),
        in_specs=[blk] * 7 + [const, const],
        out_specs=[blk, pl.BlockSpec((1, H_B, DH_B, DH_B), lambda s, g: (s, 0, 0, 0))],
        out_shape=[jax.ShapeDtypeStruct((n * t, D_B), F32),
                   jax.ShapeDtypeStruct((n, H_B, DH_B, DH_B), F32)],
        scratch_shapes=[pltpu.VMEM((PAIRS, LANES, LANES), F32)],
        compiler_params=_params(("parallel", "arbitrary")),
        name="rwkv_scan",
    )(*vecs, tri, blk_ones)


def _wkv_step_kernel(s_ref, kk_ref, w_ref, b_ref, k_ref, v_ref, c_ref, kr_ref, y_ref, so_ref, *, nh):
    sub = lax.broadcasted_iota(jnp.int32, (DH_B, DH_B), 0)
    lan = lax.broadcasted_iota(jnp.int32, (DH_B, DH_B), 1)
    diag = (sub == lan).astype(F32)

    def body(i, carry):
        s = s_ref[i]
        v = v_ref[i]
        sa = -jnp.sum(s * kk_ref[i], axis=1, keepdims=True)
        z = jnp.sum(s * c_ref[i], axis=1, keepdims=True)
        vcol = jnp.sum(diag * v, axis=1, keepdims=True)
        so_ref[i] = s * w_ref[i] + sa * b_ref[i] + vcol * k_ref[i]
        y_ref[i] = jnp.sum(diag * z, axis=0, keepdims=True) + v * kr_ref[i]
        return carry

    lax.fori_loop(0, nh, body, 0, unroll=8)


def _wkv_step(state, vecs):
    nb = state.shape[0]
    nh_total = nb * H_B
    nh = 64
    sblk = pl.BlockSpec((nh, DH_B, DH_B), lambda i: (i, 0, 0))
    vblk = pl.BlockSpec((nh, 1, DH_B), lambda i: (i, 0, 0))
    y, s_new = pl.pallas_call(
        functools.partial(_wkv_step_kernel, nh=nh),
        grid=(nh_total // nh,),
        in_specs=[sblk] + [vblk] * 7,
        out_specs=[vblk, sblk],
        out_shape=[jax.ShapeDtypeStruct((nh_total, 1, DH_B), F32),
                   jax.ShapeDtypeStruct((nh_total, DH_B, DH_B), F32)],
        compiler_params=_params(("parallel",)),
        name="wkv_step",
    )(state.reshape(nh_total, DH_B, DH_B), *[x.reshape(nh_total, 1, DH_B) for x in vecs])
    return y.reshape(nb, D_B), s_new.reshape(nb, H_B, DH_B, DH_B)


def _mix_ffn_kernel(ya_ref, y_ref, bv_ref, g_ref, lnw_ref, lnb_ref, ones_ref, h_ref, wo_ref, gmix_ref,
                    gpre_ref, gpost_ref, wg_ref, wu_ref, wd_ref, o_ref):
    ones = ones_ref[...]
    y = y_ref[...]
    mean = _segsum(y, ones) * (1.0 / DH_B)
    d = y - mean
    var = _segsum(d * d, ones) * (1.0 / DH_B)
    yn = d * lax.rsqrt(var + GN_EPS) * lnw_ref[...] + lnb_ref[...]
    yb = ((yn + bv_ref[...]) * g_ref[...]).astype(BF16)
    mix = _dot(ya_ref[...], wo_ref[0:D_A, :]) + _dot(yb, wo_ref[D_A:, :])
    h2 = h_ref[...] + _rms(mix, gmix_ref[...])
    o_ref[...] = _ffn_half_step(h2, gpre_ref, gpost_ref, wg_ref, wu_ref, wd_ref)


def _mix_ffn(ya, y, bv, g, ln_w, ln_b, ones, h, w_out, g_mix, g_pre, g_post, wg, wu, wd, tm):
    m = h.shape[0]
    row = lambda w: pl.BlockSpec((tm, w), lambda i: (i, 0))
    resident = lambda a: pl.BlockSpec(a.shape, lambda i: (0,) * a.ndim, pipeline_mode=pl.Buffered(1))
    consts = (ln_w, ln_b, ones)
    weights = (w_out, g_mix, g_pre, g_post, wg, wu, wd)
    return pl.pallas_call(
        _mix_ffn_kernel,
        grid=(m // tm,),
        in_specs=[row(D_A), row(D_B), row(D_B), row(D_B)] + [resident(a) for a in consts]
        + [row(D_MODEL)] + [resident(a) for a in weights],
        out_specs=row(D_MODEL),
        out_shape=jax.ShapeDtypeStruct((m, D_MODEL), F32),
        compiler_params=_params(("parallel",)),
        name="mix_ffn",
    )(ya, y, bv, g, *consts, h, *weights)


def _block_ones(n, seg):
    i = jnp.arange(n) // seg
    return (i[:, None] == i[None, :]).astype(BF16)


def kernel(x_prompt, x_sample, cache_k, cache_v, state_wkv, state_shift, page_table, n_ffn1_pre, n_ffn1_post, ffn1_gate, ffn1_up, ffn1_down, n_mix_pre, n_mix_post, w_in, w_out, lambda_q1, lambda_k1, lambda_q2, lambda_k2, subln, mu_shift, w0, w2, a0, a2, g2, k_k, k_a, r_k, ln_x_w, ln_x_b, n_ffn2_pre, n_ffn2_post, ffn2_gate, ffn2_up, ffn2_down):
    n_p, t_p, _ = x_prompt.shape
    n_s, t_s, _ = x_sample.shape
    assert t_s == 1
    depth = w_in.shape[0]
    n_pages = page_table.shape[1]
    past_len = n_pages * PAGE_SIZE
    ones_seg = _block_ones(D_B, DH_B)
    tab_p = _rope_tables(jnp.arange(t_p, dtype=jnp.int32))
    tab_s = _rope_tables(jnp.full((n_s,), past_len, jnp.int32))
    zeros_lora = jnp.zeros((LORA_W, D_B), F32)
    tm_p = 512
    tm_proj = 512

    yp = x_prompt.reshape(n_p * t_p, D_MODEL)
    ys = x_sample.reshape(n_s, D_MODEL)
    outs = [[] for _ in range(8)]
    for l in range(depth):
        lam_init = 0.8 - 0.6 * math.exp(-0.3 * l)
        vec = lambda a: a[l].reshape(1, -1)
        ffn1 = (vec(n_ffn1_pre), vec(n_ffn1_post), ffn1_gate[l].astype(BF16), ffn1_up[l].astype(BF16),
                ffn1_down[l].astype(BF16))
        ffn2 = (vec(n_ffn2_pre), vec(n_ffn2_post), ffn2_gate[l].astype(BF16), ffn2_up[l].astype(BF16),
                ffn2_down[l].astype(BF16))
        w_in_b = w_in[l].astype(BF16)
        w_out_b = w_out[l].astype(BF16)
        lams = (vec(lambda_q1), vec(lambda_k1), vec(lambda_q2), vec(lambda_k2))
        prep_w = (vec(mu_shift), vec(w0), vec(a0),
                  jnp.concatenate([w2[l], zeros_lora], axis=0).astype(BF16),
                  jnp.concatenate([zeros_lora, a2[l]], axis=0).astype(BF16),
                  g2[l].astype(BF16), vec(k_k), vec(k_a), r_k[l].reshape(1, D_B), ones_seg)

        def mix_tail(h, ya, y, bv, g, tm):
            return _mix_ffn(ya, y, bv, g, vec(ln_x_w), vec(ln_x_b), ones_seg, h, w_out_b, vec(n_mix_post),
                            *ffn2, tm)

        h = _ffn(yp, *ffn1, tm_p)
        q, k, v, kb, vt, pb = _proj(h, vec(n_mix_pre), w_in_b, tab_p, tm_proj, t_p // tm_proj)
        ya = _attn_prompt(q, kb, vt, lams, vec(subln), n_p, t_p, lam_init)
        prev0 = jnp.zeros((n_p, 1, SHIFT_DIM), F32)
        kk_, _, b_, k2_, v_, c_, kr_, bv_, g_, lw_ = _rwkv_prep(pb, prev0, prep_w, tm_proj, t_p)
        y, s_new = _rwkv_scan((lw_, kk_, b_, k2_, v_, c_, kr_), n_p, t_p)
        yp = mix_tail(h, ya, y, bv_, g_, tm_p)
        outs[0].append(k.reshape(n_p, t_p, H_A, 2 * DH_A))
        outs[1].append(v.reshape(n_p, t_p, H_A, 2 * DH_A))
        outs[2].append(s_new)
        outs[3].append(pb.reshape(n_p, t_p, SHIFT_DIM)[:, -1])

        h = _ffn(ys, *ffn1, n_s)
        q, k, v, _, _, pb = _proj(h, vec(n_mix_pre), w_in_b, tab_s, n_s, 1)
        ya = _attn_decode(q, k, v, cache_k, cache_v, l, page_table, lams, vec(subln), lam_init)
        kk_, w_, b_, k2_, v_, c_, kr_, bv_, g_, _ = _rwkv_prep(pb, state_shift[l], prep_w, n_s, 1)
        y, s_new = _wkv_step(state_wkv[l], (kk_, w_, b_, k2_, v_, c_, kr_))
        ys = mix_tail(h, ya, y, bv_, g_, n_s)
        outs[4].append(k.reshape(n_s, 1, H_A, 2 * DH_A))
        outs[5].append(v.reshape(n_s, 1, H_A, 2 * DH_A))
        outs[6].append(s_new)
        outs[7].append(pb)

    return (yp.reshape(n_p, t_p, D_MODEL), ys.reshape(n_s, 1, D_MODEL),
            *[jnp.stack(o) for o in outs])
```

```python
import functools
import math

import jax
import jax.numpy as jnp
from jax import lax
from jax.experimental import pallas as pl
from jax.experimental.pallas import tpu as pltpu

F32 = jnp.float32
BF16 = jnp.bfloat16

D_MODEL = 1024
H_A = 4
DH_A = 64
D_A = H_A * 2 * DH_A
ROT_DIM = DH_A // 4
ROPE_THETA = 500000.0
H_B = 8
DH_B = 64
D_B = H_B * DH_B
LORA_W = 64
LORA_A = 64
LORA_G = 128
SHIFT_DIM = 3 * D_B + LORA_W + LORA_A + LORA_G
D_IN = 3 * D_A + SHIFT_DIM
D_FF = 2816
PAGE_SIZE = 128
NORM_EPS = 1e-6
GN_EPS = 64e-5

LANES = 128
SUBLANES = 8
VMEM_LIMIT = 48 * 1024 * 1024


def _dot(a, b):
    return jnp.dot(a, b, preferred_element_type=F32)


def _dot_nt(a, b):
    return lax.dot_general(a, b, (((1,), (1,)), ((), ())), preferred_element_type=F32)


def _rms(x, g):
    return x * lax.rsqrt(jnp.mean(x * x, axis=-1, keepdims=True) + NORM_EPS) * g


def _sigmoid(x):
    return 1.0 / (1.0 + jnp.exp(-x))


def _split2(x):
    hi = x.astype(BF16)
    mid = (x - hi.astype(F32)).astype(BF16)
    return hi, mid


def _segsum(x, ones):
    hi = x.astype(BF16)
    r1 = x - hi.astype(F32)
    mid = r1.astype(BF16)
    lo = (r1 - mid.astype(F32)).astype(BF16)
    return _dot(hi, ones) + _dot(mid, ones) + _dot(lo, ones)


def _params(sem):
    return pltpu.CompilerParams(dimension_semantics=sem, vmem_limit_bytes=VMEM_LIMIT)


def _ffn_half_step(x, gpre_ref, gpost_ref, wg_ref, wu_ref, wd_ref):
    un = _rms(x, gpre_ref[...]).astype(BF16)
    g = _dot(un, wg_ref[...])
    u = _dot(un, wu_ref[...])
    hid = ((g * _sigmoid(g)) * u).astype(BF16)
    return x + 0.5 * _rms(_dot(hid, wd_ref[...]), gpost_ref[...])


def _ffn_kernel(x_ref, gpre_ref, gpost_ref, wg_ref, wu_ref, wd_ref, o_ref):
    o_ref[...] = _ffn_half_step(x_ref[...], gpre_ref, gpost_ref, wg_ref, wu_ref, wd_ref)


def _ffn(x, g_pre, g_post, wg, wu, wd, tm):
    m = x.shape[0]
    row = pl.BlockSpec((tm, D_MODEL), lambda i: (i, 0))
    vec = pl.BlockSpec((1, D_MODEL), lambda i: (0, 0))
    resident = lambda w: pl.BlockSpec(w.shape, lambda i: (0, 0), pipeline_mode=pl.Buffered(1))
    return pl.pallas_call(
        _ffn_kernel,
        grid=(m // tm,),
        in_specs=[row, vec, vec, resident(wg), resident(wu), resident(wd)],
        out_specs=row,
        out_shape=jax.ShapeDtypeStruct((m, D_MODEL), F32),
        compiler_params=_params(("parallel",)),
        name="ffn",
    )(x, g_pre, g_post, wg, wu, wd)


def _proj_kernel(h_ref, g_ref, w_ref, cos_ref, sina_ref, sinb_ref,
                 qt_ref, k_ref, v_ref, kb_ref, vt_ref, pb_ref):
    u = _rms(h_ref[...], g_ref[...]).astype(BF16)
    cos = cos_ref[...]
    sina = sina_ref[...]
    sinb = sinb_ref[...]
    half = ROT_DIM // 2

    def rope(x):
        return x * cos + pltpu.roll(x, LANES - half, 1) * sina + pltpu.roll(x, half, 1) * sinb

    qa = _dot(u, w_ref[:, 0:D_A])
    ka = _dot(u, w_ref[:, D_A:2 * D_A])
    for hh in range(H_A):
        sl = slice(hh * LANES, (hh + 1) * LANES)
        qt_ref[sl, :] = (rope(qa[:, sl]) * (DH_A ** -0.5)).T.astype(BF16)
        kh = rope(ka[:, sl])
        k_ref[:, hh, :] = kh
        kb_ref[:, sl] = kh.astype(BF16)
    va = _dot(u, w_ref[:, 2 * D_A:3 * D_A])
    for hh in range(H_A):
        v_ref[:, hh, :] = va[:, hh * LANES:(hh + 1) * LANES]
    vt_ref[...] = va.T.astype(BF16)
    pb_ref[...] = _dot(u, w_ref[:, 3 * D_A:])


def _proj(h, g, w_in, tables, tm, table_blocks):
    m = h.shape[0]
    row = lambda width: pl.BlockSpec((tm, width), lambda i: (i, 0))
    tab = pl.BlockSpec((tm, LANES), lambda i: (i % table_blocks, 0))
    per_head = pl.BlockSpec((tm, H_A, 2 * DH_A), lambda i: (i, 0, 0))
    cols = pl.BlockSpec((D_A, tm), lambda i: (0, i))
    shp = lambda width, dt: jax.ShapeDtypeStruct((m, width), dt)
    return pl.pallas_call(
        _proj_kernel,
        grid=(m // tm,),
        in_specs=[row(D_MODEL), pl.BlockSpec((1, D_MODEL), lambda i: (0, 0)),
                  pl.BlockSpec((D_MODEL, D_IN), lambda i: (0, 0)), tab, tab, tab],
        out_specs=[cols, per_head, per_head, row(D_A), cols, row(SHIFT_DIM)],
        out_shape=[jax.ShapeDtypeStruct((D_A, m), BF16), jax.ShapeDtypeStruct((m, H_A, 2 * DH_A), F32),
                   jax.ShapeDtypeStruct((m, H_A, 2 * DH_A), F32), shp(D_A, BF16),
                   jax.ShapeDtypeStruct((D_A, m), BF16), shp(SHIFT_DIM, F32)],
        compiler_params=_params(("parallel",)),
        name="proj",
    )(h, g, w_in, *tables)


def _rope_tables(pos):
    half = ROT_DIM // 2
    t = pos.shape[0]
    inv_freq = ROPE_THETA ** (-jnp.arange(half, dtype=F32) / half)
    ang = pos.astype(F32)[:, None] * inv_freq[None, :]
    cos = jnp.cos(ang)
    sin = jnp.sin(ang)
    rest = DH_A - ROT_DIM
    cos64 = jnp.concatenate([cos, cos, jnp.ones((t, rest), F32)], axis=1)
    sina64 = jnp.concatenate([-sin, jnp.zeros((t, half + rest), F32)], axis=1)
    sinb64 = jnp.concatenate([jnp.zeros((t, half), F32), sin, jnp.zeros((t, rest), F32)], axis=1)
    two = lambda x: jnp.concatenate([x, x], axis=1)
    return two(cos64), two(sina64), two(sinb64)


def _lambda(lq1_ref, lk1_ref, lq2_ref, lk2_ref, lam_init):
    s1 = jnp.sum(lq1_ref[...] * lk1_ref[...], axis=-1, keepdims=True)
    s2 = jnp.sum(lq2_ref[...] * lk2_ref[...], axis=-1, keepdims=True)
    return jnp.exp(s1) - jnp.exp(s2) + lam_init


def _attn_kernel(qt_ref, k_ref, vt_ref, lq1_ref, lk1_ref, lq2_ref, lk2_ref, subln_ref, o_ref,
                 m_ref, acc_ref, sta_ref, stb_ref, *, tq, lam_init):
    qi = pl.program_id(2)
    qt = qt_ref[...].astype(F32)
    dim = lax.broadcasted_iota(jnp.int32, qt.shape, 0)
    qs = (jnp.where(dim < DH_A, qt, 0.0).astype(BF16), jnp.where(dim >= DH_A, qt, 0.0).astype(BF16))
    m_ref[...] = jnp.full(m_ref.shape, -jnp.inf, F32)
    acc_ref[...] = jnp.zeros_like(acc_ref)
    krow = lax.broadcasted_iota(jnp.int32, (tq, tq), 0)
    qcol = lax.broadcasted_iota(jnp.int32, (tq, tq), 1)
    ones_rows = jnp.ones((acc_ref.shape[1] - LANES, tq), BF16)

    maps = range(2)

    def score(kstart, st_ref):
        k = k_ref[pl.ds(kstart, tq), :]
        for j in maps:
            st_ref[j] = _dot(k, qs[j])

    def consume(kstart, st_ref, diagonal):
        vt = jnp.concatenate([vt_ref[:, pl.ds(kstart, tq)], ones_rows], axis=0)
        st = [st_ref[j] for j in maps]
        if diagonal:
            st = [jnp.where(krow <= qcol, s, -jnp.inf) for s in st]
        m_prev = [m_ref[j] for j in maps]
        m_new = [jnp.maximum(m_prev[j], jnp.max(st[j], axis=0, keepdims=True)) for j in maps]
        p = [jnp.exp(st[j] - m_new[j]) for j in maps]
        alpha = [jnp.exp(m_prev[j] - m_new[j]) for j in maps]
        pv = [_dot(vt, p[j].astype(BF16)) for j in maps]
        for j in maps:
            acc_ref[j] = alpha[j] * acc_ref[j] + pv[j]
            m_ref[j] = m_new[j]

    start = lambda i: pl.multiple_of(i * tq, tq)
    score(0, sta_ref)

    def body(ki, carry):
        for parity, (cur, nxt) in enumerate(((sta_ref, stb_ref), (stb_ref, sta_ref))):
            @pl.when(ki % 2 == parity)
            def _(cur=cur, nxt=nxt):
                score(start(ki + 1), nxt)
                consume(start(ki), cur, False)
        return carry

    lax.fori_loop(0, qi, body, 0)
    for parity, cur in enumerate((sta_ref, stb_ref)):
        @pl.when(qi % 2 == parity)
        def _(cur=cur):
            consume(start(qi), cur, True)

    lam = _lambda(lq1_ref, lk1_ref, lq2_ref, lk2_ref, lam_init)
    norm = lambda j: acc_ref[j, 0:LANES, :] / acc_ref[j, LANES:LANES + 1, :]
    ot = norm(0) - lam * norm(1)
    o_ref[...] = (_rms(ot.T, subln_ref[...]) * (1.0 - lam_init)).astype(BF16)


def _attn_prompt(qt, kb, vt, lams, subln, n, t, lam_init):
    tq = 512
    nq = t // tq
    ospec = pl.BlockSpec((tq, LANES), lambda b, h, i: (b * nq + i, h))
    qtspec = pl.BlockSpec((LANES, tq), lambda b, h, i: (h, b * nq + i))
    kspec = pl.BlockSpec((t, LANES), lambda b, h, i: (b, h))
    vtspec = pl.BlockSpec((LANES, t), lambda b, h, i: (h, b))
    small = lambda w: pl.BlockSpec((1, w), lambda b, h, i: (0, 0))
    return pl.pallas_call(
        functools.partial(_attn_kernel, tq=tq, lam_init=lam_init),
        grid=(n, H_A, nq),
        in_specs=[qtspec, kspec, vtspec, small(DH_A), small(DH_A), small(DH_A), small(DH_A),
                  small(2 * DH_A)],
        out_specs=ospec,
        out_shape=jax.ShapeDtypeStruct((n * t, D_A), BF16),
        scratch_shapes=[pltpu.VMEM((2, 1, tq), F32), pltpu.VMEM((2, LANES + 16, tq), F32),
                        pltpu.VMEM((2, tq, tq), F32), pltpu.VMEM((2, tq, tq), F32)],
        compiler_params=_params(("parallel", "parallel", "arbitrary")),
        name="attn_prompt",
    )(qt, kb, vt, *lams, subln)


def _attn_decode_kernel(pt_ref, q_ref, ks_ref, vs_ref, lq1_ref, lk1_ref, lq2_ref, lk2_ref, subln_ref,
                        *rest, n_pages, lam_init):
    del pt_ref
    kp_refs = rest[:n_pages]
    vp_refs = rest[n_pages:2 * n_pages]
    o_ref = rest[2 * n_pages]
    nmap = 2 * H_A
    page_rows = PAGE_SIZE * H_A
    heads = lambda x: jnp.concatenate([x[:, hh * LANES:(hh + 1) * LANES] for hh in range(H_A)], axis=0)
    q4 = heads(q_ref[0].astype(F32))
    k4 = heads(ks_ref[0].astype(BF16).astype(F32))
    v4 = heads(vs_ref[0].astype(BF16).astype(F32))
    r8 = lax.broadcasted_iota(jnp.int32, (nmap, LANES), 0)
    l8 = lax.broadcasted_iota(jnp.int32, (nmap, LANES), 1)
    q8 = jnp.where((l8 // DH_A) == (r8 // H_A), jnp.concatenate([q4, q4], axis=0), 0.0)
    q8_b = q8.astype(BF16)
    s = jnp.concatenate([_dot_nt(q8_b, kp_refs[pg][...].astype(BF16)) for pg in range(n_pages)], axis=1)
    rs = lax.broadcasted_iota(jnp.int32, s.shape, 0)
    cs = lax.broadcasted_iota(jnp.int32, s.shape, 1)
    s = jnp.where((cs % H_A) == (rs % H_A), s, -jnp.inf)
    s_self = jnp.sum(q8 * jnp.concatenate([k4, k4], axis=0), axis=1, keepdims=True)
    m = jnp.maximum(jnp.max(s, axis=1, keepdims=True), s_self)
    e = jnp.exp(s - m)
    e_self = jnp.exp(s_self - m)
    inv = 1.0 / (jnp.sum(e, axis=1, keepdims=True) + e_self)
    lam = _lambda(lq1_ref, lk1_ref, lq2_ref, lk2_ref, lam_init)
    p = e * inv
    p_self = e_self * inv
    pc = (p[0:H_A] - lam * p[H_A:nmap]).astype(BF16)
    pc_self = (p_self[0:H_A] - lam * p_self[H_A:nmap]).astype(BF16).astype(F32)
    o = pc_self * v4
    for pg in range(n_pages):
        o = o + _dot(pc[:, pg * page_rows:(pg + 1) * page_rows], vp_refs[pg][...].astype(BF16))
    o = (_rms(o, subln_ref[...]) * (1.0 - lam_init)).astype(BF16)
    for hh in range(H_A):
        o_ref[0, :, hh * LANES:(hh + 1) * LANES] = o[hh:hh + 1, :]


def _attn_decode(q, k_self, v_self, cache_k, cache_v, layer, page_table, lams, subln, lam_init):
    nb, n_pages = page_table.shape
    tok = pl.BlockSpec((1, 1, D_A), lambda b, pt: (b, 0, 0))
    small = lambda w: pl.BlockSpec((1, w), lambda b, pt: (0, 0))
    as_rows = lambda c: c.reshape(c.shape[0], c.shape[1], PAGE_SIZE * H_A, 2 * DH_A)
    cache_k, cache_v = as_rows(cache_k), as_rows(cache_v)
    page = lambda p: pl.BlockSpec((None, None, PAGE_SIZE * H_A, 2 * DH_A),
                                  lambda b, pt: (layer, pt[b, p], 0, 0))
    pages = [page(p) for p in range(n_pages)]
    grid_spec = pltpu.PrefetchScalarGridSpec(
        num_scalar_prefetch=1,
        grid=(nb,),
        in_specs=[tok, tok, tok, small(DH_A), small(DH_A), small(DH_A), small(DH_A), small(2 * DH_A)]
        + pages + pages,
        out_specs=tok,
    )
    out = pl.pallas_call(
        functools.partial(_attn_decode_kernel, n_pages=n_pages, lam_init=lam_init),
        grid_spec=grid_spec,
        out_shape=jax.ShapeDtypeStruct((nb, 1, D_A), BF16),
        compiler_params=_params(("arbitrary",)),
        name="attn_decode",
    )(page_table, q.reshape(nb, 1, D_A), k_self.reshape(nb, 1, D_A), v_self.reshape(nb, 1, D_A),
      *lams, subln, *([cache_k] * n_pages), *([cache_v] * n_pages))
    return out.reshape(nb, D_A)


def _prep_kernel(*refs, tm, tiles_per_seq):
    if tiles_per_seq:
        pb_ref, prev_ref, tail_ref = refs[:3]
        refs = refs[3:]
    else:
        pb_ref, prev_ref = refs[:2]
        refs = refs[2:]
    (mu_ref, w0_ref, a0_ref, w2_ref, a2_ref, g2_ref, kk_w_ref, ka_w_ref, rk_w_ref, ones_ref,
     kk_o, w_o, b_o, k_o, v_o, c_o, kr_o, bv_o, g_o, lw_o) = refs
    pb = pb_ref[...]
    if tiles_per_seq:
        first = (pl.program_id(0) % tiles_per_seq) == 0
        prev_row = jnp.where(first, prev_ref[0], tail_ref[SUBLANES - 1:SUBLANES, :])
        rows = lax.broadcasted_iota(jnp.int32, pb.shape, 0)
        shifted = jnp.where(rows == 0, jnp.broadcast_to(prev_row, pb.shape), pltpu.roll(pb, 1, 0))
    else:
        shifted = prev_ref[...]
    xs = pb + (shifted - pb) * mu_ref[...]
    r = xs[:, 0:D_B]
    k = xs[:, D_B:2 * D_B]
    v = xs[:, 2 * D_B:3 * D_B]
    xwa = xs[:, 3 * D_B:3 * D_B + LORA_W + LORA_A]
    xg = xs[:, 3 * D_B + LORA_W + LORA_A:]
    ones = ones_ref[...]
    w_raw = w0_ref[...] + _dot(jnp.tanh(xwa).astype(BF16), w2_ref[...])
    z = -w_raw
    softplus = jnp.maximum(z, 0.0) + jnp.log(1.0 + jnp.exp(-jnp.abs(z)))
    log_decay = -jnp.exp(-softplus - 0.5)
    decay = jnp.exp(log_decay)
    a =_sigmoid(a0_ref[...] + _dot(xwa.astype(BF16), a2_ref[...]))
    g = _dot(_sigmoid(xg).astype(BF16), g2_ref[...])
    kk = k * kk_w_ref[...]
    kk = kk / jnp.maximum(jnp.sqrt(_segsum(kk * kk, ones)), 1e-12)
    k2 = k * (1.0 + (a - 1.0) * ka_w_ref[...])
    b = kk * a
    br = _segsum(b * r, ones)
    kr = _segsum(k2 * r, ones)
    bonus = _segsum(r * k2 * rk_w_ref[...], ones)
    kk_o[...] = kk
    w_o[...] = decay
    b_o[...] = b
    k_o[...] = k2
    v_o[...] = v
    c_o[...] = decay * r - kk * br
    kr_o[...] = kr
    bv_o[...] = bonus * v
    g_o[...] = g
    lw_o[...] = log_decay


def _rwkv_prep(pb, prev, weights, tm, seq_len):
    m = pb.shape[0]
    row = lambda w: pl.BlockSpec((tm, w), lambda i: (i, 0))
    const = lambda a: pl.BlockSpec(a.shape, lambda i: (0,) * a.ndim)
    if seq_len > 1:
        tiles_per_seq = seq_len // tm
        tail = pl.BlockSpec((SUBLANES, SHIFT_DIM),
                            lambda i: (jnp.maximum(i * (tm // SUBLANES) - 1, 0), 0))
        head = [row(SHIFT_DIM), pl.BlockSpec((1, 1, SHIFT_DIM), lambda i: (i // tiles_per_seq, 0, 0)), tail]
        args = [pb, prev, pb]
    else:
        tiles_per_seq = 0
        head = [row(SHIFT_DIM), row(SHIFT_DIM)]
        args = [pb, prev]
    return pl.pallas_call(
        functools.partial(_prep_kernel, tm=tm, tiles_per_seq=tiles_per_seq),
        grid=(m // tm,),
        in_specs=head + [const(a) for a in weights],
        out_specs=[row(D_B)] * 10,
        out_shape=[jax.ShapeDtypeStruct((m, D_B), F32)] * 10,
        compiler_params=_params(("parallel",)),
        name="rwkv_prep",
    )(*args, *weights)


PAIRS = H_B // 2
CHUNK = 64
GROUP = 4 * CHUNK


def _dot3(a_hi, a_mid, b_hi, b_mid):
    return _dot(a_hi, b_hi) + _dot(a_hi, b_mid) + _dot(a_mid, b_hi)


def _scan_kernel(lw_ref, kk_ref, b_ref, k_ref, v_ref, c_ref, kr_ref, tri_ref, blk_ref,
                 y_ref, sout_ref, h_ref):
    grp = pl.program_id(1)

    @pl.when(grp == 0)
    def _():
        h_ref[...] = jnp.zeros_like(h_ref)

    n = GROUP
    tri = tri_ref[...]
    blk = blk_ref[...]
    lw = lw_ref[...]
    lw_hi = lw.astype(BF16)
    lw_r = lw - lw_hi.astype(F32)
    lw_mid = lw_r.astype(BF16)
    lw_lo = (lw_r - lw_mid.astype(F32)).astype(BF16)
    cum = _dot(tri, lw_hi) + _dot(tri, lw_mid) + _dot(tri, lw_lo)
    tot = _dot(blk, lw_hi) + _dot(blk, lw_mid) + _dot(blk, lw_lo)
    g_prev = jnp.exp(cum - lw)
    g_inv = jnp.exp(-cum)
    g_end = jnp.exp(tot - cum)
    g_tot = jnp.exp(tot)
    kk = kk_ref[...]
    b = b_ref[...]
    k = k_ref[...]
    v = v_ref[...]
    at = -(kk * g_prev)
    ct = c_ref[...] * g_prev
    bt = b * g_inv
    kt = k * g_inv
    bh = b * g_end
    kh = k * g_end
    krv = kr_ref[...] * v

    ri = lax.broadcasted_iota(jnp.int32, (n, n), 0)
    ci = lax.broadcasted_iota(jnp.int32, (n, n), 1)
    first = (ri // CHUNK) * CHUNK
    stril = ((ci - first) | (ri - 1 - ci)) >= 0
    eye_n = (ri == ci).astype(F32)
    lane = lax.broadcasted_iota(jnp.int32, (n, LANES), 1)
    rown = lax.broadcasted_iota(jnp.int32, (n, LANES), 0)
    head0 = lane < DH_B
    r2 = lax.broadcasted_iota(jnp.int32, (LANES, LANES), 0)
    c2 = lax.broadcasted_iota(jnp.int32, (LANES, LANES), 1)
    same_head = (r2 // DH_B) == (c2 // DH_B)
    eye_l = r2 == c2

    heads = [(p, j) for p in range(PAIRS) for j in range(2)]
    sls = [slice(p * LANES, (p + 1) * LANES) for p in range(PAIRS)]
    bf = lambda x: x.astype(BF16)
    v_b = [bf(v[:, sl]) for sl in sls]
    bk = [bf(jnp.concatenate([bt[:, sl].T, kt[:, sl].T], axis=1)) for sl in sls]
    xs = []
    for p, j in heads:
        mine = head0 if j == 0 else lane >= DH_B
        lhs = jnp.concatenate([jnp.where(mine, at[:, sls[p]], 0.0), jnp.where(mine, ct[:, sls[p]], 0.0)], axis=0)
        xs.append(_dot(bf(lhs), bk[p]))
    lab = [jnp.where(stril, x[:n, :n], 0.0) for x in xs]
    lak = [jnp.where(stril, x[:n, n:], 0.0) for x in xs]
    mcb = [jnp.where(stril, x[n:, :n], 0.0) for x in xs]
    mck = [jnp.where(stril, x[n:, n:], 0.0) for x in xs]
    tm = [eye_n + x for x in lab]
    xp = lab
    for _ in range(5):
        xb = [bf(x) for x in xp]
        xp = [_dot(x, x) for x in xb]
        tm = [t + _dot(bf(t), bf(x)) for t, x in zip(tm, xp)]
    gm = [_dot(bf(lak[i]), v_b[p]) for i, (p, j) in enumerate(heads)]
    tag = [_dot(bf(tm[i]), bf(jnp.concatenate([at[:, sls[p]], gm[i]], axis=1)))
           for i, (p, j) in enumerate(heads)]
    mt = [_dot(bf(mcb[i]), bf(tag[i])) for i in range(len(heads))]
    mv = [_dot(bf(mck[i]), v_b[p]) for i, (p, j) in enumerate(heads)]

    ta, tg, cy, yg, bht, kht, hs = [], [], [], [], [], [], []
    for p in range(PAIRS):
        pick = lambda f: jnp.where(head0, f(2 * p), f(2 * p + 1))
        ta.append(pick(lambda i: tag[i][:, :LANES]))
        tg.append(pick(lambda i: tag[i][:, LANES:]))
        cy.append(bf(ct[:, sls[p]] + pick(lambda i: mt[i][:, :LANES])))
        yg.append(pick(lambda i: mt[i][:, LANES:] + mv[i]) + krv[:, sls[p]])
        bht.append(bf(bh[:, sls[p]].T))
        kht.append(bf(kh[:, sls[p]].T))
        hs.append(h_ref[p])
    ys = [[] for _ in range(PAIRS)]
    for cidx in range(GROUP // CHUNK):
        rows = slice(cidx * CHUNK, (cidx + 1) * CHUNK)
        in_chunk = (rown // CHUNK) == cidx
        for p in range(PAIRS):
            only = lambda x: bf(jnp.where(in_chunk, x, 0.0))
            decay_c = jnp.broadcast_to(g_tot[cidx * CHUNK:cidx * CHUNK + 1, sls[p]], (LANES, LANES))
            pm = jnp.where(eye_l, decay_c, 0.0) + jnp.where(same_head, _dot(bht[p], only(ta[p])), 0.0)
            qm = jnp.where(same_head, _dot(bht[p], only(tg[p])) + _dot(kht[p], only(v[:, sls[p]])), 0.0)
            ys[p].append(_dot(cy[p][rows], bf(hs[p])) + yg[p][rows])
            hs[p] = _dot3(*_split2(pm), *_split2(hs[p])) + qm
    for p in range(PAIRS):
        h_ref[p] = hs[p]
        y_ref[:, sls[p]] = jnp.concatenate(ys[p], axis=0)

    @pl.when(grp == pl.num_programs(1) - 1)
    def _():
        for p in range(PAIRS):
            st = h_ref[p].T
            sout_ref[0, 2 * p] = st[:DH_B, :DH_B]
            sout_ref[0, 2 * p + 1] = st[DH_B:, DH_B:]


def _rwkv_scan(vecs, n, t):
    idx = jnp.arange(GROUP)
    same = (idx[:, None] // CHUNK) == (idx[None, :] // CHUNK)
    tri = (same & (idx[None, :] <= idx[:, None])).astype(BF16)
    blk_ones = same.astype(BF16)
    groups = t // GROUP
    blk = pl.BlockSpec((GROUP, D_B), lambda s, g: (s * groups + g, 0))
    const = pl.BlockSpec((GROUP, GROUP), lambda s, g: (0, 0))
    return pl.pallas_call(
        _scan_kernel,
        grid=(n, groups),
        in_specs=[blk] * 7 + [const, const],
        out_specs=[blk, pl.BlockSpec((1, H_B, DH_B, DH_B), lambda s, g: (s, 0, 0, 0))],
        out_shape=[jax.ShapeDtypeStruct((n * t, D_B), F32),
                   jax.ShapeDtypeStruct((n, H_B, DH_B, DH_B), F32)],
        scratch_shapes=[pltpu.VMEM((PAIRS, LANES, LANES), F32)],
        compiler_params=_params(("parallel", "arbitrary")),
        name="rwkv_scan",
    )(*vecs, tri, blk_ones)


def _wkv_step_kernel(s_ref, kk_ref, w_ref, b_ref, k_ref, v_ref, c_ref, kr_ref, y_ref, so_ref, t_ref):
    hd = pl.program_id(0)

    @pl.when(hd == 0)
    def _():
        for i, ref in enumerate((kk_ref, w_ref, b_ref, k_ref, v_ref, c_ref, kr_ref)):
            t_ref[i] = ref[...].T

    base = pl.multiple_of(hd * DH_B, DH_B)
    kk, w, b, k, _, c, kr = (t_ref[i, pl.ds(base, DH_B), :] for i in range(7))
    kr_row = kr[0:1, :]

    def group(gi, carry):
        v0 = pl.multiple_of(gi * SUBLANES, SUBLANES)
        v8 = t_ref[4, pl.ds(base + v0, SUBLANES), :]
        ys = []
        for j in range(SUBLANES):
            s = s_ref[v0 + j]
            v_row = v8[j:j + 1, :]
            sa = -jnp.sum(s * kk, axis=0, keepdims=True)
            ys.append(jnp.sum(s * c, axis=0, keepdims=True) + v_row * kr_row)
            so_ref[v0 + j] = s * w + sa * b + v_row * k
        y_ref[pl.ds(v0, SUBLANES), :] = jnp.concatenate(ys, axis=0)
        return carry

    lax.fori_loop(0, DH_B // SUBLANES, group, 0)


def _wkv_step(state, vecs):
    nb = state.shape[0]
    sblk = pl.BlockSpec((None, DH_B, DH_B, nb), lambda h: (h, 0, 0, 0))
    vblk = pl.BlockSpec((nb, D_B), lambda h: (0, 0))
    y_t, s_t = pl.pallas_call(
        _wkv_step_kernel,
        grid=(H_B,),
        in_specs=[sblk] + [vblk] * 7,
        out_specs=[pl.BlockSpec((DH_B, nb), lambda h: (h, 0)), sblk],
        out_shape=[jax.ShapeDtypeStruct((D_B, nb), F32),
                   jax.ShapeDtypeStruct((H_B, DH_B, DH_B, nb), F32)],
        scratch_shapes=[pltpu.VMEM((7, D_B, nb), F32)],
        compiler_params=_params(("arbitrary",)),
        name="wkv_step",
    )(jnp.transpose(state, (1, 2, 3, 0)), *vecs)
    return y_t.T, jnp.transpose(s_t, (3, 0, 1, 2))


def _mix_ffn_kernel(ya_ref, y_ref, bv_ref, g_ref, lnw_ref, lnb_ref, ones_ref, h_ref, wo_ref, gmix_ref,
                    gpre_ref, gpost_ref, wg_ref, wu_ref, wd_ref, o_ref):
    ones = ones_ref[...]
    y = y_ref[...]
    mean = _segsum(y, ones) * (1.0 / DH_B)
    d = y - mean
    var = _segsum(d * d, ones) * (1.0 / DH_B)
    yn = d * lax.rsqrt(var + GN_EPS) * lnw_ref[...] + lnb_ref[...]
    yb = ((yn + bv_ref[...]) * g_ref[...]).astype(BF16)
    mix = _dot(ya_ref[...], wo_ref[0:D_A, :]) + _dot(yb, wo_ref[D_A:, :])
    h2 = h_ref[...] + _rms(mix, gmix_ref[...])
    o_ref[...] = _ffn_half_step(h2, gpre_ref, gpost_ref, wg_ref, wu_ref, wd_ref)


def _mix_ffn(ya, y, bv, g, ln_w, ln_b, ones, h, w_out, g_mix, g_pre, g_post, wg, wu, wd, tm):
    m = h.shape[0]
    row = lambda w: pl.BlockSpec((tm, w), lambda i: (i, 0))
    resident = lambda a: pl.BlockSpec(a.shape, lambda i: (0,) * a.ndim, pipeline_mode=pl.Buffered(1))
    consts = (ln_w, ln_b, ones)
    weights = (w_out, g_mix, g_pre, g_post, wg, wu, wd)
    return pl.pallas_call(
        _mix_ffn_kernel,
        grid=(m // tm,),
        in_specs=[row(D_A), row(D_B), row(D_B), row(D_B)] + [resident(a) for a in consts]
        + [row(D_MODEL)] + [resident(a) for a in weights],
        out_specs=row(D_MODEL),
        out_shape=jax.ShapeDtypeStruct((m, D_MODEL), F32),
        compiler_params=_params(("parallel",)),
        name="mix_ffn",
    )(ya, y, bv, g, *consts, h, *weights)


def _block_ones(n, seg):
    i = jnp.arange(n) // seg
    return (i[:, None] == i[None, :]).astype(BF16)


def kernel(x_prompt, x_sample, cache_k, cache_v, state_wkv, state_shift, page_table, n_ffn1_pre, n_ffn1_post, ffn1_gate, ffn1_up, ffn1_down, n_mix_pre, n_mix_post, w_in, w_out, lambda_q1, lambda_k1, lambda_q2, lambda_k2, subln, mu_shift, w0, w2, a0, a2, g2, k_k, k_a, r_k, ln_x_w, ln_x_b, n_ffn2_pre, n_ffn2_post, ffn2_gate, ffn2_up, ffn2_down):
    n_p, t_p, _ = x_prompt.shape
    n_s, t_s, _ = x_sample.shape
    assert t_s == 1
    depth = w_in.shape[0]
    n_pages = page_table.shape[1]
    past_len = n_pages * PAGE_SIZE
    ones_seg = _block_ones(D_B, DH_B)
    tab_p = _rope_tables(jnp.arange(t_p, dtype=jnp.int32))
    tab_s = _rope_tables(jnp.full((n_s,), past_len, jnp.int32))
    zeros_lora = jnp.zeros((LORA_W, D_B), F32)
    tm_p = 512
    tm_proj = 512

    yp = x_prompt.reshape(n_p * t_p, D_MODEL)
    ys = x_sample.reshape(n_s, D_MODEL)
    outs = [[] for _ in range(8)]
    for l in range(depth):
        lam_init = 0.8 - 0.6 * math.exp(-0.3 * l)
        vec = lambda a: a[l].reshape(1, -1)
        ffn1 = (vec(n_ffn1_pre), vec(n_ffn1_post), ffn1_gate[l].astype(BF16), ffn1_up[l].astype(BF16),
                ffn1_down[l].astype(BF16))
        ffn2 = (vec(n_ffn2_pre), vec(n_ffn2_post), ffn2_gate[l].astype(BF16), ffn2_up[l].astype(BF16),
                ffn2_down[l].astype(BF16))
        w_in_b = w_in[l].astype(BF16)
        w_out_b = w_out[l].astype(BF16)
        lams = (vec(lambda_q1), vec(lambda_k1), vec(lambda_q2), vec(lambda_k2))
        prep_w = (vec(mu_shift), vec(w0), vec(a0),
                  jnp.concatenate([w2[l], zeros_lora], axis=0).astype(BF16),
                  jnp.concatenate([zeros_lora, a2[l]], axis=0).astype(BF16),
                  g2[l].astype(BF16), vec(k_k), vec(k_a), r_k[l].reshape(1, D_B), ones_seg)

        def mix_tail(h, ya, y, bv, g, tm):
            return _mix_ffn(ya, y, bv, g, vec(ln_x_w), vec(ln_x_b), ones_seg, h, w_out_b, vec(n_mix_post),
                            *ffn2, tm)

        h = _ffn(yp, *ffn1, tm_p)
        qt, k, v, kb, vt, pb = _proj(h, vec(n_mix_pre), w_in_b, tab_p, tm_proj, t_p // tm_proj)
        ya = _attn_prompt(qt, kb, vt, lams, vec(subln), n_p, t_p, lam_init)
        prev0 = jnp.zeros((n_p, 1, SHIFT_DIM), F32)
        kk_, _, b_, k2_, v_, c_, kr_, bv_, g_, lw_ = _rwkv_prep(pb, prev0, prep_w, tm_proj, t_p)
        y, s_new = _rwkv_scan((lw_, kk_, b_, k2_, v_, c_, kr_), n_p, t_p)
        yp = mix_tail(h, ya, y, bv_, g_, tm_p)
        outs[0].append(k.reshape(n_p, t_p, H_A, 2 * DH_A))
        outs[1].append(v.reshape(n_p, t_p, H_A, 2 * DH_A))
        outs[2].append(s_new)
        outs[3].append(pb.reshape(n_p, t_p, SHIFT_DIM)[:, -1])

        h = _ffn(ys, *ffn1, n_s)
        qt, k, v, _, _, pb = _proj(h, vec(n_mix_pre), w_in_b, tab_s, n_s, 1)
        ya = _attn_decode(qt.T, k, v, cache_k, cache_v, l, page_table, lams, vec(subln), lam_init)
        kk_, w_, b_, k2_, v_, c_, kr_, bv_, g_, _ = _rwkv_prep(pb, state_shift[l], prep_w, n_s, 1)
        y, s_new = _wkv_step(state_wkv[l], (kk_, w_, b_, k2_, v_, c_, kr_))
        ys = mix_tail(h, ya, y, bv_, g_, n_s)
        outs[4].append(k.reshape(n_s, 1, H_A, 2 * DH_A))
        outs[5].append(v.reshape(n_s, 1, H_A, 2 * DH_A))
        outs[6].append(s_new)
        outs[7].append(pb)

    return (yp.reshape(n_p, t_p, D_MODEL), ys.reshape(n_s, 1, D_MODEL),
            *[jnp.stack(o) for o in outs])
```

```python
import functools
import math

import jax
import jax.numpy as jnp
from jax import lax
from jax.experimental import pallas as pl
from jax.experimental.pallas import tpu as pltpu

F32 = jnp.float32
BF16 = jnp.bfloat16

D_MODEL = 1024
H_A = 4
DH_A = 64
D_A = H_A * 2 * DH_A
ROT_DIM = DH_A // 4
ROPE_THETA = 500000.0
H_B = 8
DH_B = 64
D_B = H_B * DH_B
LORA_W = 64
LORA_A = 64
LORA_G = 128
SHIFT_DIM = 3 * D_B + LORA_W + LORA_A + LORA_G
D_IN = 3 * D_A + SHIFT_DIM
D_FF = 2816
PAGE_SIZE = 128
NORM_EPS = 1e-6
GN_EPS = 64e-5

LANES = 128
SUBLANES = 8
VMEM_LIMIT = 48 * 1024 * 1024


def _dot(a, b):
    return jnp.dot(a, b, preferred_element_type=F32)


def _dot_nt(a, b):
    return lax.dot_general(a, b, (((1,), (1,)), ((), ())), preferred_element_type=F32)


def _rms(x, g):
    return x * lax.rsqrt(jnp.mean(x * x, axis=-1, keepdims=True) + NORM_EPS) * g


def _sigmoid(x):
    return 1.0 / (1.0 + jnp.exp(-x))


def _split2(x):
    hi = x.astype(BF16)
    mid = (x - hi.astype(F32)).astype(BF16)
    return hi, mid


def _segsum(x, ones):
    hi = x.astype(BF16)
    r1 = x - hi.astype(F32)
    mid = r1.astype(BF16)
    lo = (r1 - mid.astype(F32)).astype(BF16)
    return _dot(hi, ones) + _dot(mid, ones) + _dot(lo, ones)


def _params(sem):
    return pltpu.CompilerParams(dimension_semantics=sem, vmem_limit_bytes=VMEM_LIMIT)


def _ffn_half_step(x, gpre_ref, gpost_ref, wg_ref, wu_ref, wd_ref):
    un = _rms(x, gpre_ref[...]).astype(BF16)
    g = _dot(un, wg_ref[...])
    u = _dot(un, wu_ref[...])
    hid = ((g * _sigmoid(g)) * u).astype(BF16)
    return x + 0.5 * _rms(_dot(hid, wd_ref[...]), gpost_ref[...])


def _ffn_kernel(x_ref, gpre_ref, gpost_ref, wg_ref, wu_ref, wd_ref, o_ref):
    o_ref[...] = _ffn_half_step(x_ref[...], gpre_ref, gpost_ref, wg_ref, wu_ref, wd_ref)


def _ffn(x, g_pre, g_post, wg, wu, wd, tm):
    m = x.shape[0]
    row = pl.BlockSpec((tm, D_MODEL), lambda i: (i, 0))
    vec = pl.BlockSpec((1, D_MODEL), lambda i: (0, 0))
    resident = lambda w: pl.BlockSpec(w.shape, lambda i: (0, 0), pipeline_mode=pl.Buffered(1))
    return pl.pallas_call(
        _ffn_kernel,
        grid=(m // tm,),
        in_specs=[row, vec, vec, resident(wg), resident(wu), resident(wd)],
        out_specs=row,
        out_shape=jax.ShapeDtypeStruct((m, D_MODEL), F32),
        compiler_params=_params(("parallel",)),
        name="ffn",
    )(x, g_pre, g_post, wg, wu, wd)


def _proj_kernel(h_ref, g_ref, w_ref, cos_ref, sina_ref, sinb_ref,
                 qt_ref, k_ref, v_ref, kb_ref, vt_ref, pb_ref):
    u = _rms(h_ref[...], g_ref[...]).astype(BF16)
    cos = cos_ref[...]
    sina = sina_ref[...]
    sinb = sinb_ref[...]
    half = ROT_DIM // 2

    def rope(x):
        return x * cos + pltpu.roll(x, LANES - half, 1) * sina + pltpu.roll(x, half, 1) * sinb

    qa = _dot(u, w_ref[:, 0:D_A])
    ka = _dot(u, w_ref[:, D_A:2 * D_A])
    for hh in range(H_A):
        sl = slice(hh * LANES, (hh + 1) * LANES)
        qt_ref[sl, :] = (rope(qa[:, sl]) * (DH_A ** -0.5)).T.astype(BF16)
        kh = rope(ka[:, sl])
        k_ref[:, hh, :] = kh
        kb_ref[:, sl] = kh.astype(BF16)
    va = _dot(u, w_ref[:, 2 * D_A:3 * D_A])
    for hh in range(H_A):
        v_ref[:, hh, :] = va[:, hh * LANES:(hh + 1) * LANES]
    vt_ref[...] = va.T.astype(BF16)
    pb_ref[...] = _dot(u, w_ref[:, 3 * D_A:])


def _proj(h, g, w_in, tables, tm, table_blocks):
    m = h.shape[0]
    row = lambda width: pl.BlockSpec((tm, width), lambda i: (i, 0))
    tab = pl.BlockSpec((tm, LANES), lambda i: (i % table_blocks, 0))
    per_head = pl.BlockSpec((tm, H_A, 2 * DH_A), lambda i: (i, 0, 0))
    cols = pl.BlockSpec((D_A, tm), lambda i: (0, i))
    shp = lambda width, dt: jax.ShapeDtypeStruct((m, width), dt)
    return pl.pallas_call(
        _proj_kernel,
        grid=(m // tm,),
        in_specs=[row(D_MODEL), pl.BlockSpec((1, D_MODEL), lambda i: (0, 0)),
                  pl.BlockSpec((D_MODEL, D_IN), lambda i: (0, 0)), tab, tab, tab],
        out_specs=[cols, per_head, per_head, row(D_A), cols, row(SHIFT_DIM)],
        out_shape=[jax.ShapeDtypeStruct((D_A, m), BF16), jax.ShapeDtypeStruct((m, H_A, 2 * DH_A), F32),
                   jax.ShapeDtypeStruct((m, H_A, 2 * DH_A), F32), shp(D_A, BF16),
                   jax.ShapeDtypeStruct((D_A, m), BF16), shp(SHIFT_DIM, F32)],
        compiler_params=_params(("parallel",)),
        name="proj",
    )(h, g, w_in, *tables)


def _rope_tables(pos):
    half = ROT_DIM // 2
    t = pos.shape[0]
    inv_freq = ROPE_THETA ** (-jnp.arange(half, dtype=F32) / half)
    ang = pos.astype(F32)[:, None] * inv_freq[None, :]
    cos = jnp.cos(ang)
    sin = jnp.sin(ang)
    rest = DH_A - ROT_DIM
    cos64 = jnp.concatenate([cos, cos, jnp.ones((t, rest), F32)], axis=1)
    sina64 = jnp.concatenate([-sin, jnp.zeros((t, half + rest), F32)], axis=1)
    sinb64 = jnp.concatenate([jnp.zeros((t, half), F32), sin, jnp.zeros((t, rest), F32)], axis=1)
    two = lambda x: jnp.concatenate([x, x], axis=1)
    return two(cos64), two(sina64), two(sinb64)


def _lambda(lq1_ref, lk1_ref, lq2_ref, lk2_ref, lam_init):
    s1 = jnp.sum(lq1_ref[...] * lk1_ref[...], axis=-1, keepdims=True)
    s2 = jnp.sum(lq2_ref[...] * lk2_ref[...], axis=-1, keepdims=True)
    return jnp.exp(s1) - jnp.exp(s2) + lam_init


def _attn_kernel(qt_ref, k_ref, vt_ref, lq1_ref, lk1_ref, lq2_ref, lk2_ref, subln_ref, o_ref,
                 m_ref, acc_ref, sta_ref, stb_ref, *, tq, lam_init):
    krow = lax.broadcasted_iota(jnp.int32, (tq, tq), 0)
    qcol = lax.broadcasted_iota(jnp.int32, (tq, tq), 1)
    dim = lax.broadcasted_iota(jnp.int32, (LANES, tq), 0)
    ones_rows = jnp.ones((acc_ref.shape[1] - LANES, tq), BF16)
    lam = _lambda(lq1_ref, lk1_ref, lq2_ref, lk2_ref, lam_init)
    maps = range(2)
    start = lambda i: pl.multiple_of(i * tq, tq)

    def score(qs, kstart, st_ref):
        k = k_ref[pl.ds(kstart, tq), :]
        for j in maps:
            st_ref[j] = _dot(k, qs[j])

    def consume(kstart, st_ref, diagonal):
        vt = jnp.concatenate([vt_ref[:, pl.ds(kstart, tq)], ones_rows], axis=0)
        st = [st_ref[j] for j in maps]
        if diagonal:
            st = [jnp.where(krow <= qcol, s, -jnp.inf) for s in st]
        m_prev = [m_ref[j] for j in maps]
        m_new = [jnp.maximum(m_prev[j], jnp.max(st[j], axis=0, keepdims=True)) for j in maps]
        p = [jnp.exp(st[j] - m_new[j]) for j in maps]
        alpha = [jnp.exp(m_prev[j] - m_new[j]) for j in maps]
        pv = [_dot(vt, p[j].astype(BF16)) for j in maps]
        for j in maps:
            acc_ref[j] = alpha[j] * acc_ref[j] + pv[j]
            m_ref[j] = m_new[j]

    def tile(qi, carry):
        m_ref[...] = jnp.full(m_ref.shape, -jnp.inf, F32)
        acc_ref[...] = jnp.zeros_like(acc_ref)
        qt = qt_ref[:, pl.ds(start(qi), tq)].astype(F32)
        qs = (jnp.where(dim < DH_A, qt, 0.0).astype(BF16), jnp.where(dim >= DH_A, qt, 0.0).astype(BF16))
        score(qs, 0, sta_ref)

        def body(ki, c):
            for parity, (cur, nxt) in enumerate(((sta_ref, stb_ref), (stb_ref, sta_ref))):
                @pl.when(ki % 2 == parity)
                def _(cur=cur, nxt=nxt):
                    score(qs, start(ki + 1), nxt)
                    consume(start(ki), cur, False)
            return c

        lax.fori_loop(0, qi, body, 0)
        for parity, cur in enumerate((sta_ref, stb_ref)):
            @pl.when(qi % 2 == parity)
            def _(cur=cur):
                consume(start(qi), cur, True)

        norm = lambda j: acc_ref[j, 0:LANES, :] / acc_ref[j, LANES:LANES + 1, :]
        ot = norm(0) - lam * norm(1)
        o_ref[pl.ds(start(qi), tq), :] = (_rms(ot.T, subln_ref[...]) * (1.0 - lam_init)).astype(BF16)
        return carry

    lax.fori_loop(0, qt_ref.shape[1] // tq, tile, 0)


def _attn_prompt(qt, kb, vt, lams, subln, n, t, lam_init):
    tq = 512
    rows = pl.BlockSpec((t, LANES), lambda b, h: (b, h))
    cols = pl.BlockSpec((LANES, t), lambda b, h: (h, b))
    small = lambda w: pl.BlockSpec((1, w), lambda b, h: (0, 0))
    return pl.pallas_call(
        functools.partial(_attn_kernel, tq=tq, lam_init=lam_init),
        grid=(n, H_A),
        in_specs=[cols, rows, cols, small(DH_A), small(DH_A), small(DH_A), small(DH_A), small(2 * DH_A)],
        out_specs=rows,
        out_shape=jax.ShapeDtypeStruct((n * t, D_A), BF16),
        scratch_shapes=[pltpu.VMEM((2, 1, tq), F32), pltpu.VMEM((2, LANES + 16, tq), F32),
                        pltpu.VMEM((2, tq, tq), F32), pltpu.VMEM((2, tq, tq), F32)],
        compiler_params=_params(("parallel", "parallel")),
        name="attn_prompt",
    )(qt, kb, vt, *lams, subln)


def _attn_decode_kernel(pt_ref, q_ref, ks_ref, vs_ref, lq1_ref, lk1_ref, lq2_ref, lk2_ref, subln_ref,
                        *rest, n_pages, lam_init):
    del pt_ref
    kp_refs = rest[:n_pages]
    vp_refs = rest[n_pages:2 * n_pages]
    o_ref = rest[2 * n_pages]
    nmap = 2 * H_A
    page_rows = PAGE_SIZE * H_A
    heads = lambda x: jnp.concatenate([x[:, hh * LANES:(hh + 1) * LANES] for hh in range(H_A)], axis=0)
    q4 = heads(q_ref[0].astype(F32))
    k4 = heads(ks_ref[0].astype(BF16).astype(F32))
    v4 = heads(vs_ref[0].astype(BF16).astype(F32))
    r8 = lax.broadcasted_iota(jnp.int32, (nmap, LANES), 0)
    l8 = lax.broadcasted_iota(jnp.int32, (nmap, LANES), 1)
    q8 = jnp.where((l8 // DH_A) == (r8 // H_A), jnp.concatenate([q4, q4], axis=0), 0.0)
    q8_b = q8.astype(BF16)
    s = jnp.concatenate([_dot_nt(q8_b, kp_refs[pg][...].astype(BF16)) for pg in range(n_pages)], axis=1)
    rs = lax.broadcasted_iota(jnp.int32, s.shape, 0)
    cs = lax.broadcasted_iota(jnp.int32, s.shape, 1)
    s = jnp.where((cs % H_A) == (rs % H_A), s, -jnp.inf)
    s_self = jnp.sum(q8 * jnp.concatenate([k4, k4], axis=0), axis=1, keepdims=True)
    m = jnp.maximum(jnp.max(s, axis=1, keepdims=True), s_self)
    e = jnp.exp(s - m)
    e_self = jnp.exp(s_self - m)
    inv = 1.0 / (jnp.sum(e, axis=1, keepdims=True) + e_self)
    lam = _lambda(lq1_ref, lk1_ref, lq2_ref, lk2_ref, lam_init)
    p = e * inv
    p_self = e_self * inv
    pc = (p[0:H_A] - lam * p[H_A:nmap]).astype(BF16)
    pc_self = (p_self[0:H_A] - lam * p_self[H_A:nmap]).astype(BF16).astype(F32)
    o = pc_self * v4
    for pg in range(n_pages):
        o = o + _dot(pc[:, pg * page_rows:(pg + 1) * page_rows], vp_refs[pg][...].astype(BF16))
    o = (_rms(o, subln_ref[...]) * (1.0 - lam_init)).astype(BF16)
    for hh in range(H_A):
        o_ref[0, :, hh * LANES:(hh + 1) * LANES] = o[hh:hh + 1, :]


def _attn_decode(q, k_self, v_self, cache_k, cache_v, layer, page_table, lams, subln, lam_init):
    nb, n_pages = page_table.shape
    tok = pl.BlockSpec((1, 1, D_A), lambda b, pt: (b, 0, 0))
    small = lambda w: pl.BlockSpec((1, w), lambda b, pt: (0, 0))
    as_rows = lambda c: c.reshape(c.shape[0], c.shape[1], PAGE_SIZE * H_A, 2 * DH_A)
    cache_k, cache_v = as_rows(cache_k), as_rows(cache_v)
    page = lambda p: pl.BlockSpec((None, None, PAGE_SIZE * H_A, 2 * DH_A),
                                  lambda b, pt: (layer, pt[b, p], 0, 0))
    pages = [page(p) for p in range(n_pages)]
    grid_spec = pltpu.PrefetchScalarGridSpec(
        num_scalar_prefetch=1,
        grid=(nb,),
        in_specs=[tok, tok, tok, small(DH_A), small(DH_A), small(DH_A), small(DH_A), small(2 * DH_A)]
        + pages + pages,
        out_specs=tok,
    )
    out = pl.pallas_call(
        functools.partial(_attn_decode_kernel, n_pages=n_pages, lam_init=lam_init),
        grid_spec=grid_spec,
        out_shape=jax.ShapeDtypeStruct((nb, 1, D_A), BF16),
        compiler_params=_params(("arbitrary",)),
        name="attn_decode",
    )(page_table, q.reshape(nb, 1, D_A), k_self.reshape(nb, 1, D_A), v_self.reshape(nb, 1, D_A),
      *lams, subln, *([cache_k] * n_pages), *([cache_v] * n_pages))
    return out.reshape(nb, D_A)


def _prep_kernel(*refs, tm, tiles_per_seq):
    if tiles_per_seq:
        pb_ref, prev_ref, tail_ref = refs[:3]
        refs = refs[3:]
    else:
        pb_ref, prev_ref = refs[:2]
        refs = refs[2:]
    (mu_ref, w0_ref, a0_ref, w2_ref, a2_ref, g2_ref, kk_w_ref, ka_w_ref, rk_w_ref, ones_ref,
     kk_o, w_o, b_o, k_o, v_o, c_o, kr_o, bv_o, g_o) = refs
    pb = pb_ref[...]
    if tiles_per_seq:
        first = (pl.program_id(0) % tiles_per_seq) == 0
        prev_row = jnp.where(first, prev_ref[0], tail_ref[SUBLANES - 1:SUBLANES, :])
        rows = lax.broadcasted_iota(jnp.int32, pb.shape, 0)
        shifted = jnp.where(rows == 0, jnp.broadcast_to(prev_row, pb.shape), pltpu.roll(pb, 1, 0))
    else:
        shifted = prev_ref[...]
    xs = pb + (shifted - pb) * mu_ref[...]
    r = xs[:, 0:D_B]
    k = xs[:, D_B:2 * D_B]
    v = xs[:, 2 * D_B:3 * D_B]
    xwa = xs[:, 3 * D_B:3 * D_B + LORA_W + LORA_A]
    xg = xs[:, 3 * D_B + LORA_W + LORA_A:]
    ones = ones_ref[...]
    w_raw = w0_ref[...] + _dot(jnp.tanh(xwa).astype(BF16), w2_ref[...])
    z = -w_raw
    softplus = jnp.maximum(z, 0.0) + jnp.log(1.0 + jnp.exp(-jnp.abs(z)))
    log_decay = -jnp.exp(-softplus - 0.5)
    decay = jnp.exp(log_decay)
    a =_sigmoid(a0_ref[...] + _dot(xwa.astype(BF16), a2_ref[...]))
    g = _dot(_sigmoid(xg).astype(BF16), g2_ref[...])
    kk = k * kk_w_ref[...]
    kk = kk / jnp.maximum(jnp.sqrt(_segsum(kk * kk, ones)), 1e-12)
    k2 = k * (1.0 + (a - 1.0) * ka_w_ref[...])
    b = kk * a
    br = _segsum(b * r, ones)
    kr = _segsum(k2 * r, ones)
    bonus = _segsum(r * k2 * rk_w_ref[...], ones)
    kk_o[...] = kk
    w_o[...] = log_decay if tiles_per_seq else decay
    b_o[...] = b
    k_o[...] = k2
    v_o[...] = v
    c_o[...] = decay * r - kk * br
    kr_o[...] = kr
    bv_o[...] = bonus * v
    g_o[...] = g


def _rwkv_prep(pb, prev, weights, tm, seq_len):
    m = pb.shape[0]
    row = lambda w: pl.BlockSpec((tm, w), lambda i: (i, 0))
    const = lambda a: pl.BlockSpec(a.shape, lambda i: (0,) * a.ndim)
    if seq_len > 1:
        tiles_per_seq = seq_len // tm
        tail = pl.BlockSpec((SUBLANES, SHIFT_DIM),
                            lambda i: (jnp.maximum(i * (tm // SUBLANES) - 1, 0), 0))
        head = [row(SHIFT_DIM), pl.BlockSpec((1, 1, SHIFT_DIM), lambda i: (i // tiles_per_seq, 0, 0)), tail]
        args = [pb, prev, pb]
    else:
        tiles_per_seq = 0
        head = [row(SHIFT_DIM), row(SHIFT_DIM)]
        args = [pb, prev]
    return pl.pallas_call(
        functools.partial(_prep_kernel, tm=tm, tiles_per_seq=tiles_per_seq),
        grid=(m // tm,),
        in_specs=head + [const(a) for a in weights],
        out_specs=[row(D_B)] * 9,
        out_shape=[jax.ShapeDtypeStruct((m, D_B), F32)] * 9,
        compiler_params=_params(("parallel",)),
        name="rwkv_prep",
    )(*args, *weights)


PAIRS = H_B // 2
CHUNK = 64
GROUP = 4 * CHUNK


def _dot3(a_hi, a_mid, b_hi, b_mid):
    return _dot(a_hi, b_hi) + _dot(a_hi, b_mid) + _dot(a_mid, b_hi)


def _scan_kernel(lw_ref, kk_ref, b_ref, k_ref, v_ref, c_ref, kr_ref, tri_ref, blk_ref,
                 y_ref, sout_ref, h_ref):
    grp = pl.program_id(1)

    @pl.when(grp == 0)
    def _():
        h_ref[...] = jnp.zeros_like(h_ref)

    n = GROUP
    tri = tri_ref[...]
    blk = blk_ref[...]
    lw = lw_ref[...]
    lw_hi = lw.astype(BF16)
    lw_r = lw - lw_hi.astype(F32)
    lw_mid = lw_r.astype(BF16)
    lw_lo = (lw_r - lw_mid.astype(F32)).astype(BF16)
    cum = _dot(tri, lw_hi) + _dot(tri, lw_mid) + _dot(tri, lw_lo)
    tot = _dot(blk, lw_hi) + _dot(blk, lw_mid) + _dot(blk, lw_lo)
    g_prev = jnp.exp(cum - lw)
    g_inv = jnp.exp(-cum)
    g_end = jnp.exp(tot - cum)
    g_tot = jnp.exp(tot)
    kk = kk_ref[...]
    b = b_ref[...]
    k = k_ref[...]
    v = v_ref[...]
    at = -(kk * g_prev)
    ct = c_ref[...] * g_prev
    bt = b * g_inv
    kt = k * g_inv
    bh = b * g_end
    kh = k * g_end
    krv = kr_ref[...] * v

    ri = lax.broadcasted_iota(jnp.int32, (n, n), 0)
    ci = lax.broadcasted_iota(jnp.int32, (n, n), 1)
    first = (ri // CHUNK) * CHUNK
    stril = ((ci - first) | (ri - 1 - ci)) >= 0
    eye_n = (ri == ci).astype(F32)
    lane = lax.broadcasted_iota(jnp.int32, (n, LANES), 1)
    rown = lax.broadcasted_iota(jnp.int32, (n, LANES), 0)
    head0 = lane < DH_B
    r2 = lax.broadcasted_iota(jnp.int32, (LANES, LANES), 0)
    c2 = lax.broadcasted_iota(jnp.int32, (LANES, LANES), 1)
    same_head = (r2 // DH_B) == (c2 // DH_B)
    eye_l = r2 == c2

    heads = [(p, j) for p in range(PAIRS) for j in range(2)]
    sls = [slice(p * LANES, (p + 1) * LANES) for p in range(PAIRS)]
    bf = lambda x: x.astype(BF16)
    v_b = [bf(v[:, sl]) for sl in sls]
    bk = [bf(jnp.concatenate([bt[:, sl].T, kt[:, sl].T], axis=1)) for sl in sls]
    xs = []
    for p, j in heads:
        mine = head0 if j == 0 else lane >= DH_B
        lhs = jnp.concatenate([jnp.where(mine, at[:, sls[p]], 0.0), jnp.where(mine, ct[:, sls[p]], 0.0)], axis=0)
        xs.append(_dot(bf(lhs), bk[p]))
    lab = [jnp.where(stril, x[:n, :n], 0.0) for x in xs]
    lak = [jnp.where(stril, x[:n, n:], 0.0) for x in xs]
    mcb = [jnp.where(stril, x[n:, :n], 0.0) for x in xs]
    mck = [jnp.where(stril, x[n:, n:], 0.0) for x in xs]
    tm = [eye_n + x for x in lab]
    xb = [bf(x) for x in lab]
    xb = [bf(_dot(x, x)) for x in xb]
    for _ in range(4):
        both = [_dot(jnp.concatenate([bf(t), x], axis=0), x) for t, x in zip(tm, xb)]
        tm = [t + r[:n] for t, r in zip(tm, both)]
        xb = [bf(r[n:]) for r in both]
    tm = [t + _dot(bf(t), x) for t, x in zip(tm, xb)]
    gm = [_dot(bf(lak[i]), v_b[p]) for i, (p, j) in enumerate(heads)]
    tag = [_dot(bf(tm[i]), bf(jnp.concatenate([at[:, sls[p]], gm[i]], axis=1)))
           for i, (p, j) in enumerate(heads)]
    mt = [_dot(bf(mcb[i]), bf(tag[i])) for i in range(len(heads))]
    mv = [_dot(bf(mck[i]), v_b[p]) for i, (p, j) in enumerate(heads)]

    ta, tg, cy, yg, bht, kht, hs = [], [], [], [], [], [], []
    for p in range(PAIRS):
        pick = lambda f: jnp.where(head0, f(2 * p), f(2 * p + 1))
        ta.append(pick(lambda i: tag[i][:, :LANES]))
        tg.append(pick(lambda i: tag[i][:, LANES:]))
        cy.append(bf(ct[:, sls[p]] + pick(lambda i: mt[i][:, :LANES])))
        yg.append(pick(lambda i: mt[i][:, LANES:] + mv[i]) + krv[:, sls[p]])
        bht.append(bf(bh[:, sls[p]].T))
        kht.append(bf(kh[:, sls[p]].T))
        hs.append(h_ref[p])
    ys = [[] for _ in range(PAIRS)]
    for cidx in range(GROUP // CHUNK):
        rows = slice(cidx * CHUNK, (cidx + 1) * CHUNK)
        in_chunk = (rown // CHUNK) == cidx
        for p in range(PAIRS):
            only = lambda x: bf(jnp.where(in_chunk, x, 0.0))
            decay_c = jnp.broadcast_to(g_tot[cidx * CHUNK:cidx * CHUNK + 1, sls[p]], (LANES, LANES))
            pm = jnp.where(eye_l, decay_c, 0.0) + jnp.where(same_head, _dot(bht[p], only(ta[p])), 0.0)
            qm = jnp.where(same_head, _dot(bht[p], only(tg[p])) + _dot(kht[p], only(v[:, sls[p]])), 0.0)
            ys[p].append(_dot(cy[p][rows], bf(hs[p])) + yg[p][rows])
            hs[p] = _dot3(*_split2(pm), *_split2(hs[p])) + qm
    for p in range(PAIRS):
        h_ref[p] = hs[p]
        y_ref[:, sls[p]] = jnp.concatenate(ys[p], axis=0)

    @pl.when(grp == pl.num_programs(1) - 1)
    def _():
        for p in range(PAIRS):
            st = h_ref[p].T
            sout_ref[0, 2 * p] = st[:DH_B, :DH_B]
            sout_ref[0, 2 * p + 1] = st[DH_B:, DH_B:]


def _rwkv_scan(vecs, n, t):
    idx = jnp.arange(GROUP)
    same = (idx[:, None] // CHUNK) == (idx[None, :] // CHUNK)
    tri = (same & (idx[None, :] <= idx[:, None])).astype(BF16)
    blk_ones = same.astype(BF16)
    groups = t // GROUP
    blk = pl.BlockSpec((GROUP, D_B), lambda s, g: (s * groups + g, 0))
    const = pl.BlockSpec((GROUP, GROUP), lambda s, g: (0, 0))
    return pl.pallas_call(
        _scan_kernel,
        grid=(n, groups),
        in_specs=[blk] * 7 + [const, const],
        out_specs=[blk, pl.BlockSpec((1, H_B, DH_B, DH_B), lambda s, g: (s, 0, 0, 0))],
        out_shape=[jax.ShapeDtypeStruct((n * t, D_B), F32),
                   jax.ShapeDtypeStruct((n, H_B, DH_B, DH_B), F32)],
        scratch_shapes=[pltpu.VMEM((PAIRS, LANES, LANES), F32)],
        compiler_params=_params(("parallel", "arbitrary")),
        name="rwkv_scan",
    )(*vecs, tri, blk_ones)


def _wkv_step_kernel(s_ref, kk_ref, w_ref, b_ref, k_ref, v_ref, c_ref, kr_ref, y_ref, so_ref, t_ref):
    hd = pl.program_id(0)

    @pl.when(hd == 0)
    def _():
        for i, ref in enumerate((kk_ref, w_ref, b_ref, k_ref, v_ref, c_ref, kr_ref)):
            t_ref[i] = ref[...].T

    base = pl.multiple_of(hd * DH_B, DH_B)
    kk, w, b, k, _, c, kr = (t_ref[i, pl.ds(base, DH_B), :] for i in range(7))
    kr_row = kr[0:1, :]

    def group(gi, carry):
        v0 = pl.multiple_of(gi * SUBLANES, SUBLANES)
        v8 = t_ref[4, pl.ds(base + v0, SUBLANES), :]
        ys = []
        for j in range(SUBLANES):
            s = s_ref[v0 + j]
            v_row = v8[j:j + 1, :]
            sa = -jnp.sum(s * kk, axis=0, keepdims=True)
            ys.append(jnp.sum(s * c, axis=0, keepdims=True) + v_row * kr_row)
            so_ref[v0 + j] = s * w + sa * b + v_row * k
        y_ref[pl.ds(v0, SUBLANES), :] = jnp.concatenate(ys, axis=0)
        return carry

    lax.fori_loop(0, DH_B // SUBLANES, group, 0)


def _wkv_step(state, vecs):
    nb = state.shape[0]
    sblk = pl.BlockSpec((None, DH_B, DH_B, nb), lambda h: (h, 0, 0, 0))
    vblk = pl.BlockSpec((nb, D_B), lambda h: (0, 0))
    y_t, s_t = pl.pallas_call(
        _wkv_step_kernel,
        grid=(H_B,),
        in_specs=[sblk] + [vblk] * 7,
        out_specs=[pl.BlockSpec((DH_B, nb), lambda h: (h, 0)), sblk],
        out_shape=[jax.ShapeDtypeStruct((D_B, nb), F32),
                   jax.ShapeDtypeStruct((H_B, DH_B, DH_B, nb), F32)],
        scratch_shapes=[pltpu.VMEM((7, D_B, nb), F32)],
        compiler_params=_params(("arbitrary",)),
        name="wkv_step",
    )(jnp.transpose(state, (1, 2, 3, 0)), *vecs)
    return y_t.T, jnp.transpose(s_t, (3, 0, 1, 2))


def _mix_ffn_kernel(ya_ref, y_ref, bv_ref, g_ref, lnw_ref, lnb_ref, ones_ref, h_ref, wo_ref, gmix_ref,
                    gpre_ref, gpost_ref, wg_ref, wu_ref, wd_ref, o_ref):
    ones = ones_ref[...]
    y = y_ref[...]
    mean = _segsum(y, ones) * (1.0 / DH_B)
    d = y - mean
    var = _segsum(d * d, ones) * (1.0 / DH_B)
    yn = d * lax.rsqrt(var + GN_EPS) * lnw_ref[...] + lnb_ref[...]
    yb = ((yn + bv_ref[...]) * g_ref[...]).astype(BF16)
    mix = _dot(ya_ref[...], wo_ref[0:D_A, :]) + _dot(yb, wo_ref[D_A:, :])
    h2 = h_ref[...] + _rms(mix, gmix_ref[...])
    o_ref[...] = _ffn_half_step(h2, gpre_ref, gpost_ref, wg_ref, wu_ref, wd_ref)


def _mix_ffn(ya, y, bv, g, ln_w, ln_b, ones, h, w_out, g_mix, g_pre, g_post, wg, wu, wd, tm):
    m = h.shape[0]
    row = lambda w: pl.BlockSpec((tm, w), lambda i: (i, 0))
    resident = lambda a: pl.BlockSpec(a.shape, lambda i: (0,) * a.ndim, pipeline_mode=pl.Buffered(1))
    consts = (ln_w, ln_b, ones)
    weights = (w_out, g_mix, g_pre, g_post, wg, wu, wd)
    return pl.pallas_call(
        _mix_ffn_kernel,
        grid=(m // tm,),
        in_specs=[row(D_A), row(D_B), row(D_B), row(D_B)] + [resident(a) for a in consts]
        + [row(D_MODEL)] + [resident(a) for a in weights],
        out_specs=row(D_MODEL),
        out_shape=jax.ShapeDtypeStruct((m, D_MODEL), F32),
        compiler_params=_params(("parallel",)),
        name="mix_ffn",
    )(ya, y, bv, g, *consts, h, *weights)


def _block_ones(n, seg):
    i = jnp.arange(n) // seg
    return (i[:, None] == i[None, :]).astype(BF16)


def kernel(x_prompt, x_sample, cache_k, cache_v, state_wkv, state_shift, page_table, n_ffn1_pre, n_ffn1_post, ffn1_gate, ffn1_up, ffn1_down, n_mix_pre, n_mix_post, w_in, w_out, lambda_q1, lambda_k1, lambda_q2, lambda_k2, subln, mu_shift, w0, w2, a0, a2, g2, k_k, k_a, r_k, ln_x_w, ln_x_b, n_ffn2_pre, n_ffn2_post, ffn2_gate, ffn2_up, ffn2_down):
    n_p, t_p, _ = x_prompt.shape
    n_s, t_s, _ = x_sample.shape
    assert t_s == 1
    depth = w_in.shape[0]
    n_pages = page_table.shape[1]
    past_len = n_pages * PAGE_SIZE
    ones_seg = _block_ones(D_B, DH_B)
    tab_p = _rope_tables(jnp.arange(t_p, dtype=jnp.int32))
    tab_s = _rope_tables(jnp.full((n_s,), past_len, jnp.int32))
    zeros_lora = jnp.zeros((LORA_W, D_B), F32)
    tm_p = 512
    tm_proj = 512

    yp = x_prompt.reshape(n_p * t_p, D_MODEL)
    ys = x_sample.reshape(n_s, D_MODEL)
    outs = [[] for _ in range(8)]
    for l in range(depth):
        lam_init = 0.8 - 0.6 * math.exp(-0.3 * l)
        vec = lambda a: a[l].reshape(1, -1)
        ffn1 = (vec(n_ffn1_pre), vec(n_ffn1_post), ffn1_gate[l].astype(BF16), ffn1_up[l].astype(BF16),
                ffn1_down[l].astype(BF16))
        ffn2 = (vec(n_ffn2_pre), vec(n_ffn2_post), ffn2_gate[l].astype(BF16), ffn2_up[l].astype(BF16),
                ffn2_down[l].astype(BF16))
        w_in_b = w_in[l].astype(BF16)
        w_out_b = w_out[l].astype(BF16)
        lams = (vec(lambda_q1), vec(lambda_k1), vec(lambda_q2), vec(lambda_k2))
        prep_w = (vec(mu_shift), vec(w0), vec(a0),
                  jnp.concatenate([w2[l], zeros_lora], axis=0).astype(BF16),
                  jnp.concatenate([zeros_lora, a2[l]], axis=0).astype(BF16),
                  g2[l].astype(BF16), vec(k_k), vec(k_a), r_k[l].reshape(1, D_B), ones_seg)

        def mix_tail(h, ya, y, bv, g, tm):
            return _mix_ffn(ya, y, bv, g, vec(ln_x_w), vec(ln_x_b), ones_seg, h, w_out_b, vec(n_mix_post),
                            *ffn2, tm)

        h = _ffn(yp, *ffn1, tm_p)
        qt, k, v, kb, vt, pb = _proj(h, vec(n_mix_pre), w_in_b, tab_p, tm_proj, t_p // tm_proj)
        ya = _attn_prompt(qt, kb, vt, lams, vec(subln), n_p, t_p, lam_init)
        prev0 = jnp.zeros((n_p, 1, SHIFT_DIM), F32)
        kk_, lw_, b_, k2_, v_, c_, kr_, bv_, g_ = _rwkv_prep(pb, prev0, prep_w, tm_proj, t_p)
        y, s_new = _rwkv_scan((lw_, kk_, b_, k2_, v_, c_, kr_), n_p, t_p)
        yp = mix_tail(h, ya, y, bv_, g_, tm_p)
        outs[0].append(k.reshape(n_p, t_p, H_A, 2 * DH_A))
        outs[1].append(v.reshape(n_p, t_p, H_A, 2 * DH_A))
        outs[2].append(s_new)
        outs[3].append(pb.reshape(n_p, t_p, SHIFT_DIM)[:, -1])

        h = _ffn(ys, *ffn1, n_s)
        qt, k, v, _, _, pb = _proj(h, vec(n_mix_pre), w_in_b, tab_s, n_s, 1)
        ya = _attn_decode(qt.T, k, v, cache_k, cache_v, l, page_table, lams, vec(subln), lam_init)
        kk_, w_, b_, k2_, v_, c_, kr_, bv_, g_ = _rwkv_prep(pb, state_shift[l], prep_w, n_s, 1)
        y, s_new = _wkv_step(state_wkv[l], (kk_, w_, b_, k2_, v_, c_, kr_))
        ys = mix_tail(h, ya, y, bv_, g_, n_s)
        outs[4].append(k.reshape(n_s, 1, H_A, 2 * DH_A))
        outs[5].append(v.reshape(n_s, 1, H_A, 2 * DH_A))
        outs[6].append(s_new)
        outs[7].append(pb)

    return (yp.reshape(n_p, t_p, D_MODEL), ys.reshape(n_s, 1, D_MODEL),
            *[jnp.stack(o) for o in outs])
```

```python
import functools
import math

import jax
import jax.numpy as jnp
from jax import lax
from jax.experimental import pallas as pl
from jax.experimental.pallas import tpu as pltpu

F32 = jnp.float32
BF16 = jnp.bfloat16

D_MODEL = 1024
H_A = 4
DH_A = 64
D_A = H_A * 2 * DH_A
ROT_DIM = DH_A // 4
ROPE_THETA = 500000.0
H_B = 8
DH_B = 64
D_B = H_B * DH_B
LORA_W = 64
LORA_A = 64
LORA_G = 128
SHIFT_DIM = 3 * D_B + LORA_W + LORA_A + LORA_G
D_IN = 3 * D_A + SHIFT_DIM
D_FF = 2816
PAGE_SIZE = 128
NORM_EPS = 1e-6
GN_EPS = 64e-5

LANES = 128
SUBLANES = 8
VMEM_LIMIT = 48 * 1024 * 1024


def _dot(a, b):
    return jnp.dot(a, b, preferred_element_type=F32)


def _dot_nt(a, b):
    return lax.dot_general(a, b, (((1,), (1,)), ((), ())), preferred_element_type=F32)


def _rms(x, g):
    return x * lax.rsqrt(jnp.mean(x * x, axis=-1, keepdims=True) + NORM_EPS) * g


def _sigmoid(x):
    return 1.0 / (1.0 + jnp.exp(-x))


def _split2(x):
    hi = x.astype(BF16)
    mid = (x - hi.astype(F32)).astype(BF16)
    return hi, mid


def _segsum(x, ones):
    hi, mid = _split2(x)
    return _dot(hi, ones) + _dot(mid, ones)


def _params(sem):
    return pltpu.CompilerParams(dimension_semantics=sem, vmem_limit_bytes=VMEM_LIMIT)


def _ffn_half_step(x, gpre_ref, gpost_ref, wg_ref, wu_ref, wd_ref):
    un = _rms(x, gpre_ref[...]).astype(BF16)
    g = _dot(un, wg_ref[...])
    u = _dot(un, wu_ref[...])
    hid = ((g * _sigmoid(g)) * u).astype(BF16)
    return x + 0.5 * _rms(_dot(hid, wd_ref[...]), gpost_ref[...])


def _ffn_kernel(x_ref, gpre_ref, gpost_ref, wg_ref, wu_ref, wd_ref, o_ref):
    o_ref[...] = _ffn_half_step(x_ref[...], gpre_ref, gpost_ref, wg_ref, wu_ref, wd_ref)


def _ffn(x, g_pre, g_post, wg, wu, wd, tm):
    m = x.shape[0]
    row = pl.BlockSpec((tm, D_MODEL), lambda i: (i, 0))
    vec = pl.BlockSpec((1, D_MODEL), lambda i: (0, 0))
    resident = lambda w: pl.BlockSpec(w.shape, lambda i: (0, 0), pipeline_mode=pl.Buffered(1))
    return pl.pallas_call(
        _ffn_kernel,
        grid=(m // tm,),
        in_specs=[row, vec, vec, resident(wg), resident(wu), resident(wd)],
        out_specs=row,
        out_shape=jax.ShapeDtypeStruct((m, D_MODEL), F32),
        compiler_params=_params(("parallel",)),
        name="ffn",
    )(x, g_pre, g_post, wg, wu, wd)


def _proj_kernel(h_ref, g_ref, w_ref, cos_ref, sina_ref, sinb_ref,
                 qt_ref, k_ref, v_ref, kb_ref, vt_ref, pb_ref):
    u = _rms(h_ref[...], g_ref[...]).astype(BF16)
    cos = cos_ref[...]
    sina = sina_ref[...]
    sinb = sinb_ref[...]
    half = ROT_DIM // 2

    def rope(x):
        return x * cos + pltpu.roll(x, LANES - half, 1) * sina + pltpu.roll(x, half, 1) * sinb

    qa = _dot(u, w_ref[:, 0:D_A])
    ka = _dot(u, w_ref[:, D_A:2 * D_A])
    for hh in range(H_A):
        sl = slice(hh * LANES, (hh + 1) * LANES)
        qt_ref[sl, :] = (rope(qa[:, sl]) * (DH_A ** -0.5)).T.astype(BF16)
        kh = rope(ka[:, sl])
        k_ref[:, hh, :] = kh
        kb_ref[:, sl] = kh.astype(BF16)
    va = _dot(u, w_ref[:, 2 * D_A:3 * D_A])
    for hh in range(H_A):
        v_ref[:, hh, :] = va[:, hh * LANES:(hh + 1) * LANES]
    vt_ref[...] = va.T.astype(BF16)
    pb_ref[...] = _dot(u, w_ref[:, 3 * D_A:])


def _proj(h, g, w_in, tables, tm, table_blocks):
    m = h.shape[0]
    row = lambda width: pl.BlockSpec((tm, width), lambda i: (i, 0))
    tab = pl.BlockSpec((tm, LANES), lambda i: (i % table_blocks, 0))
    per_head = pl.BlockSpec((tm, H_A, 2 * DH_A), lambda i: (i, 0, 0))
    cols = pl.BlockSpec((D_A, tm), lambda i: (0, i))
    shp = lambda width, dt: jax.ShapeDtypeStruct((m, width), dt)
    return pl.pallas_call(
        _proj_kernel,
        grid=(m // tm,),
        in_specs=[row(D_MODEL), pl.BlockSpec((1, D_MODEL), lambda i: (0, 0)),
                  pl.BlockSpec((D_MODEL, D_IN), lambda i: (0, 0)), tab, tab, tab],
        out_specs=[cols, per_head, per_head, row(D_A), cols, row(SHIFT_DIM)],
        out_shape=[jax.ShapeDtypeStruct((D_A, m), BF16), jax.ShapeDtypeStruct((m, H_A, 2 * DH_A), F32),
                   jax.ShapeDtypeStruct((m, H_A, 2 * DH_A), F32), shp(D_A, BF16),
                   jax.ShapeDtypeStruct((D_A, m), BF16), shp(SHIFT_DIM, F32)],
        compiler_params=_params(("parallel",)),
        name="proj",
    )(h, g, w_in, *tables)


def _rope_tables(pos):
    half = ROT_DIM // 2
    t = pos.shape[0]
    inv_freq = ROPE_THETA ** (-jnp.arange(half, dtype=F32) / half)
    ang = pos.astype(F32)[:, None] * inv_freq[None, :]
    cos = jnp.cos(ang)
    sin = jnp.sin(ang)
    rest = DH_A - ROT_DIM
    cos64 = jnp.concatenate([cos, cos, jnp.ones((t, rest), F32)], axis=1)
    sina64 = jnp.concatenate([-sin, jnp.zeros((t, half + rest), F32)], axis=1)
    sinb64 = jnp.concatenate([jnp.zeros((t, half), F32), sin, jnp.zeros((t, rest), F32)], axis=1)
    two = lambda x: jnp.concatenate([x, x], axis=1)
    return two(cos64), two(sina64), two(sinb64)


def _lambda(lq1_ref, lk1_ref, lq2_ref, lk2_ref, lam_init):
    s1 = jnp.sum(lq1_ref[...] * lk1_ref[...], axis=-1, keepdims=True)
    s2 = jnp.sum(lq2_ref[...] * lk2_ref[...], axis=-1, keepdims=True)
    return jnp.exp(s1) - jnp.exp(s2) + lam_init


def _attn_kernel(qt_ref, k_ref, vt_ref, lq1_ref, lk1_ref, lq2_ref, lk2_ref, subln_ref, o_ref,
                 m_ref, acc_ref, sta_ref, stb_ref, *, tq, lam_init):
    krow = lax.broadcasted_iota(jnp.int32, (tq, tq), 0)
    qcol = lax.broadcasted_iota(jnp.int32, (tq, tq), 1)
    dim = lax.broadcasted_iota(jnp.int32, (LANES, tq), 0)
    ones_rows = jnp.ones((acc_ref.shape[1] - LANES, tq), BF16)
    lam = _lambda(lq1_ref, lk1_ref, lq2_ref, lk2_ref, lam_init)
    maps = range(2)
    start = lambda i: pl.multiple_of(i * tq, tq)

    def score(qs, kstart, st_ref):
        k = k_ref[pl.ds(kstart, tq), :]
        for j in maps:
            st_ref[j] = _dot(k, qs[j])

    def consume(kstart, st_ref, diagonal):
        vt = jnp.concatenate([vt_ref[:, pl.ds(kstart, tq)], ones_rows], axis=0)
        st = [st_ref[j] for j in maps]
        if diagonal:
            st = [jnp.where(krow <= qcol, s, -jnp.inf) for s in st]
        m_prev = [m_ref[j] for j in maps]
        m_new = [jnp.maximum(m_prev[j], jnp.max(st[j], axis=0, keepdims=True)) for j in maps]
        p = [jnp.exp(st[j] - m_new[j]) for j in maps]
        alpha = [jnp.exp(m_prev[j] - m_new[j]) for j in maps]
        pv = [_dot(vt, p[j].astype(BF16)) for j in maps]
        for j in maps:
            acc_ref[j] = alpha[j] * acc_ref[j] + pv[j]
            m_ref[j] = m_new[j]

    def tile(qi, carry):
        m_ref[...] = jnp.full(m_ref.shape, -jnp.inf, F32)
        acc_ref[...] = jnp.zeros_like(acc_ref)
        qt = qt_ref[:, pl.ds(start(qi), tq)].astype(F32)
        qs = (jnp.where(dim < DH_A, qt, 0.0).astype(BF16), jnp.where(dim >= DH_A, qt, 0.0).astype(BF16))
        score(qs, 0, sta_ref)

        def body(ki, c):
            for parity, (cur, nxt) in enumerate(((sta_ref, stb_ref), (stb_ref, sta_ref))):
                @pl.when(ki % 2 == parity)
                def _(cur=cur, nxt=nxt):
                    score(qs, start(ki + 1), nxt)
                    consume(start(ki), cur, False)
            return c

        lax.fori_loop(0, qi, body, 0)
        for parity, cur in enumerate((sta_ref, stb_ref)):
            @pl.when(qi % 2 == parity)
            def _(cur=cur):
                consume(start(qi), cur, True)

        norm = lambda j: acc_ref[j, 0:LANES, :] / acc_ref[j, LANES:LANES + 1, :]
        ot = norm(0) - lam * norm(1)
        o_ref[pl.ds(start(qi), tq), :] = (_rms(ot.T, subln_ref[...]) * (1.0 - lam_init)).astype(BF16)
        return carry

    lax.fori_loop(0, qt_ref.shape[1] // tq, tile, 0)


def _attn_prompt(qt, kb, vt, lams, subln, n, t, lam_init):
    tq = 512
    rows = pl.BlockSpec((t, LANES), lambda b, h: (b, h))
    cols = pl.BlockSpec((LANES, t), lambda b, h: (h, b))
    small = lambda w: pl.BlockSpec((1, w), lambda b, h: (0, 0))
    return pl.pallas_call(
        functools.partial(_attn_kernel, tq=tq, lam_init=lam_init),
        grid=(n, H_A),
        in_specs=[cols, rows, cols, small(DH_A), small(DH_A), small(DH_A), small(DH_A), small(2 * DH_A)],
        out_specs=rows,
        out_shape=jax.ShapeDtypeStruct((n * t, D_A), BF16),
        scratch_shapes=[pltpu.VMEM((2, 1, tq), F32), pltpu.VMEM((2, LANES + 16, tq), F32),
                        pltpu.VMEM((2, tq, tq), F32), pltpu.VMEM((2, tq, tq), F32)],
        compiler_params=_params(("parallel", "parallel")),
        name="attn_prompt",
    )(qt, kb, vt, *lams, subln)


def _attn_decode_kernel(pt_ref, q_ref, ks_ref, vs_ref, lq1_ref, lk1_ref, lq2_ref, lk2_ref, subln_ref,
                        *rest, n_pages, lam_init):
    del pt_ref
    kp_refs = rest[:n_pages]
    vp_refs = rest[n_pages:2 * n_pages]
    o_ref = rest[2 * n_pages]
    nmap = 2 * H_A
    page_rows = PAGE_SIZE * H_A
    heads = lambda x: jnp.concatenate([x[:, hh * LANES:(hh + 1) * LANES] for hh in range(H_A)], axis=0)
    q4 = heads(q_ref[0].astype(F32))
    k4 = heads(ks_ref[0].astype(BF16).astype(F32))
    v4 = heads(vs_ref[0].astype(BF16).astype(F32))
    r8 = lax.broadcasted_iota(jnp.int32, (nmap, LANES), 0)
    l8 = lax.broadcasted_iota(jnp.int32, (nmap, LANES), 1)
    q8 = jnp.where((l8 // DH_A) == (r8 // H_A), jnp.concatenate([q4, q4], axis=0), 0.0)
    q8_b = q8.astype(BF16)
    s = jnp.concatenate([_dot_nt(q8_b, kp_refs[pg][...].astype(BF16)) for pg in range(n_pages)], axis=1)
    rs = lax.broadcasted_iota(jnp.int32, s.shape, 0)
    cs = lax.broadcasted_iota(jnp.int32, s.shape, 1)
    s = jnp.where((cs % H_A) == (rs % H_A), s, -jnp.inf)
    s_self = jnp.sum(q8 * jnp.concatenate([k4, k4], axis=0), axis=1, keepdims=True)
    m = jnp.maximum(jnp.max(s, axis=1, keepdims=True), s_self)
    e = jnp.exp(s - m)
    e_self = jnp.exp(s_self - m)
    inv = 1.0 / (jnp.sum(e, axis=1, keepdims=True) + e_self)
    lam = _lambda(lq1_ref, lk1_ref, lq2_ref, lk2_ref, lam_init)
    p = e * inv
    p_self = e_self * inv
    pc = (p[0:H_A] - lam * p[H_A:nmap]).astype(BF16)
    pc_self = (p_self[0:H_A] - lam * p_self[H_A:nmap]).astype(BF16).astype(F32)
    o = pc_self * v4
    for pg in range(n_pages):
        o = o + _dot(pc[:, pg * page_rows:(pg + 1) * page_rows], vp_refs[pg][...].astype(BF16))
    o = (_rms(o, subln_ref[...]) * (1.0 - lam_init)).astype(BF16)
    for hh in range(H_A):
        o_ref[0, :, hh * LANES:(hh + 1) * LANES] = o[hh:hh + 1, :]


def _attn_decode(q, k_self, v_self, cache_k, cache_v, layer, page_table, lams, subln, lam_init):
    nb, n_pages = page_table.shape
    tok = pl.BlockSpec((1, 1, D_A), lambda b, pt: (b, 0, 0))
    small = lambda w: pl.BlockSpec((1, w), lambda b, pt: (0, 0))
    as_rows = lambda c: c.reshape(c.shape[0], c.shape[1], PAGE_SIZE * H_A, 2 * DH_A)
    cache_k, cache_v = as_rows(cache_k), as_rows(cache_v)
    page = lambda p: pl.BlockSpec((None, None, PAGE_SIZE * H_A, 2 * DH_A),
                                  lambda b, pt: (layer, pt[b, p], 0, 0))
    pages = [page(p) for p in range(n_pages)]
    grid_spec = pltpu.PrefetchScalarGridSpec(
        num_scalar_prefetch=1,
        grid=(nb,),
        in_specs=[tok, tok, tok, small(DH_A), small(DH_A), small(DH_A), small(DH_A), small(2 * DH_A)]
        + pages + pages,
        out_specs=tok,
    )
    out = pl.pallas_call(
        functools.partial(_attn_decode_kernel, n_pages=n_pages, lam_init=lam_init),
        grid_spec=grid_spec,
        out_shape=jax.ShapeDtypeStruct((nb, 1, D_A), BF16),
        compiler_params=_params(("arbitrary",)),
        name="attn_decode",
    )(page_table, q.reshape(nb, 1, D_A), k_self.reshape(nb, 1, D_A), v_self.reshape(nb, 1, D_A),
      *lams, subln, *([cache_k] * n_pages), *([cache_v] * n_pages))
    return out.reshape(nb, D_A)


def _prep_kernel(*refs, tm, tiles_per_seq):
    if tiles_per_seq:
        pb_ref, prev_ref, tail_ref = refs[:3]
        refs = refs[3:]
    else:
        pb_ref, prev_ref = refs[:2]
        refs = refs[2:]
    (mu_ref, w0_ref, a0_ref, w2_ref, a2_ref, g2_ref, kk_w_ref, ka_w_ref, rk_w_ref, ones_ref,
     kk_o, w_o, b_o, k_o, v_o, c_o, kr_o, bv_o, g_o) = refs
    pb = pb_ref[...]
    if tiles_per_seq:
        first = (pl.program_id(0) % tiles_per_seq) == 0
        prev_row = jnp.where(first, prev_ref[0], tail_ref[SUBLANES - 1:SUBLANES, :])
        rows = lax.broadcasted_iota(jnp.int32, pb.shape, 0)
        shifted = jnp.where(rows == 0, jnp.broadcast_to(prev_row, pb.shape), pltpu.roll(pb, 1, 0))
    else:
        shifted = prev_ref[...]
    xs = pb + (shifted - pb) * mu_ref[...]
    r = xs[:, 0:D_B]
    k = xs[:, D_B:2 * D_B]
    v = xs[:, 2 * D_B:3 * D_B]
    xwa = xs[:, 3 * D_B:3 * D_B + LORA_W + LORA_A]
    xg = xs[:, 3 * D_B + LORA_W + LORA_A:]
    ones = ones_ref[...]
    w_raw = w0_ref[...] + _dot(jnp.tanh(xwa).astype(BF16), w2_ref[...])
    z = -w_raw
    softplus = jnp.maximum(z, 0.0) + jnp.log(1.0 + jnp.exp(-jnp.abs(z)))
    log_decay = -jnp.exp(-softplus - 0.5)
    decay = jnp.exp(log_decay)
    a =_sigmoid(a0_ref[...] + _dot(xwa.astype(BF16), a2_ref[...]))
    g = _dot(_sigmoid(xg).astype(BF16), g2_ref[...])
    kk = k * kk_w_ref[...]
    kk = kk / jnp.maximum(jnp.sqrt(_segsum(kk * kk, ones)), 1e-12)
    k2 = k * (1.0 + (a - 1.0) * ka_w_ref[...])
    b = kk * a
    br = _segsum(b * r, ones)
    kr = _segsum(k2 * r, ones)
    bonus = _segsum(r * k2 * rk_w_ref[...], ones)
    kk_o[...] = kk
    w_o[...] = log_decay if tiles_per_seq else decay
    b_o[...] = b
    k_o[...] = k2
    v_o[...] = v
    c_o[...] = decay * r - kk * br
    kr_o[...] = kr
    bv_o[...] = bonus * v
    g_o[...] = g


def _rwkv_prep(pb, prev, weights, tm, seq_len):
    m = pb.shape[0]
    row = lambda w: pl.BlockSpec((tm, w), lambda i: (i, 0))
    const = lambda a: pl.BlockSpec(a.shape, lambda i: (0,) * a.ndim)
    if seq_len > 1:
        tiles_per_seq = seq_len // tm
        tail = pl.BlockSpec((SUBLANES, SHIFT_DIM),
                            lambda i: (jnp.maximum(i * (tm // SUBLANES) - 1, 0), 0))
        head = [row(SHIFT_DIM), pl.BlockSpec((1, 1, SHIFT_DIM), lambda i: (i // tiles_per_seq, 0, 0)), tail]
        args = [pb, prev, pb]
    else:
        tiles_per_seq = 0
        head = [row(SHIFT_DIM), row(SHIFT_DIM)]
        args = [pb, prev]
    return pl.pallas_call(
        functools.partial(_prep_kernel, tm=tm, tiles_per_seq=tiles_per_seq),
        grid=(m // tm,),
        in_specs=head + [const(a) for a in weights],
        out_specs=[row(D_B)] * 9,
        out_shape=[jax.ShapeDtypeStruct((m, D_B), F32)] * 9,
        compiler_params=_params(("parallel",)),
        name="rwkv_prep",
    )(*args, *weights)


PAIRS = H_B // 2
CHUNK = 64
GROUP = 4 * CHUNK


def _dot3(a_hi, a_mid, b_hi, b_mid):
    return _dot(a_hi, b_hi) + _dot(a_hi, b_mid) + _dot(a_mid, b_hi)


def _scan_kernel(lw_ref, kk_ref, b_ref, k_ref, v_ref, c_ref, kr_ref, tri_ref, blk_ref,
                 y_ref, sout_ref, h_ref):
    grp = pl.program_id(1)

    @pl.when(grp == 0)
    def _():
        h_ref[...] = jnp.zeros_like(h_ref)

    n = GROUP
    tri = tri_ref[...]
    blk = blk_ref[...]
    lw = lw_ref[...]
    lw_hi = lw.astype(BF16)
    lw_r = lw - lw_hi.astype(F32)
    lw_mid = lw_r.astype(BF16)
    lw_lo = (lw_r - lw_mid.astype(F32)).astype(BF16)
    cum = _dot(tri, lw_hi) + _dot(tri, lw_mid) + _dot(tri, lw_lo)
    tot = _dot(blk, lw_hi) + _dot(blk, lw_mid) + _dot(blk, lw_lo)
    g_prev = jnp.exp(cum - lw)
    g_inv = jnp.exp(-cum)
    g_end = jnp.exp(tot - cum)
    g_tot = jnp.exp(tot)
    kk = kk_ref[...]
    b = b_ref[...]
    k = k_ref[...]
    v = v_ref[...]
    at = -(kk * g_prev)
    ct = c_ref[...] * g_prev
    bt = b * g_inv
    kt = k * g_inv
    bh = b * g_end
    kh = k * g_end
    krv = kr_ref[...] * v

    ri = lax.broadcasted_iota(jnp.int32, (n, n), 0)
    ci = lax.broadcasted_iota(jnp.int32, (n, n), 1)
    first = (ri // CHUNK) * CHUNK
    stril = ((ci - first) | (ri - 1 - ci)) >= 0
    eye_n = (ri == ci).astype(F32)
    lane = lax.broadcasted_iota(jnp.int32, (n, LANES), 1)
    rown = lax.broadcasted_iota(jnp.int32, (n, LANES), 0)
    head0 = lane < DH_B
    r2 = lax.broadcasted_iota(jnp.int32, (LANES, LANES), 0)
    c2 = lax.broadcasted_iota(jnp.int32, (LANES, LANES), 1)
    same_head = (r2 // DH_B) == (c2 // DH_B)
    eye_l = r2 == c2

    heads = [(p, j) for p in range(PAIRS) for j in range(2)]
    sls = [slice(p * LANES, (p + 1) * LANES) for p in range(PAIRS)]
    bf = lambda x: x.astype(BF16)
    v_b = [bf(v[:, sl]) for sl in sls]
    bk = [bf(jnp.concatenate([bt[:, sl].T, kt[:, sl].T], axis=1)) for sl in sls]
    xs = []
    for p, j in heads:
        mine = head0 if j == 0 else lane >= DH_B
        lhs = jnp.concatenate([jnp.where(mine, at[:, sls[p]], 0.0), jnp.where(mine, ct[:, sls[p]], 0.0)], axis=0)
        xs.append(_dot(bf(lhs), bk[p]))
    lab = [jnp.where(stril, x[:n, :n], 0.0) for x in xs]
    lak = [jnp.where(stril, x[:n, n:], 0.0) for x in xs]
    mcb = [jnp.where(stril, x[n:, :n], 0.0) for x in xs]
    mck = [jnp.where(stril, x[n:, n:], 0.0) for x in xs]
    tm = [eye_n + x for x in lab]
    xb = [bf(x) for x in lab]
    xb = [bf(_dot(x, x)) for x in xb]
    for _ in range(4):
        both = [_dot(jnp.concatenate([bf(t), x], axis=0), x) for t, x in zip(tm, xb)]
        tm = [t + r[:n] for t, r in zip(tm, both)]
        xb = [bf(r[n:]) for r in both]
    tm = [t + _dot(bf(t), x) for t, x in zip(tm, xb)]
    gm = [_dot(bf(lak[i]), v_b[p]) for i, (p, j) in enumerate(heads)]
    tag = [_dot(bf(tm[i]), bf(jnp.concatenate([at[:, sls[p]], gm[i]], axis=1)))
           for i, (p, j) in enumerate(heads)]
    mt = [_dot(bf(mcb[i]), bf(tag[i])) for i in range(len(heads))]
    mv = [_dot(bf(mck[i]), v_b[p]) for i, (p, j) in enumerate(heads)]

    ta, tg, cy, yg, bht, kht, hs = [], [], [], [], [], [], []
    for p in range(PAIRS):
        pick = lambda f: jnp.where(head0, f(2 * p), f(2 * p + 1))
        ta.append(pick(lambda i: tag[i][:, :LANES]))
        tg.append(pick(lambda i: tag[i][:, LANES:]))
        cy.append(bf(ct[:, sls[p]] + pick(lambda i: mt[i][:, :LANES])))
        yg.append(pick(lambda i: mt[i][:, LANES:] + mv[i]) + krv[:, sls[p]])
        bht.append(bf(bh[:, sls[p]].T))
        kht.append(bf(kh[:, sls[p]].T))
        hs.append(h_ref[p])
    ys = [[] for _ in range(PAIRS)]
    for cidx in range(GROUP // CHUNK):
        rows = slice(cidx * CHUNK, (cidx + 1) * CHUNK)
        in_chunk = (rown // CHUNK) == cidx
        for p in range(PAIRS):
            only = lambda x: bf(jnp.where(in_chunk, x, 0.0))
            decay_c = jnp.broadcast_to(g_tot[cidx * CHUNK:cidx * CHUNK + 1, sls[p]], (LANES, LANES))
            pm = jnp.where(eye_l, decay_c, 0.0) + jnp.where(same_head, _dot(bht[p], only(ta[p])), 0.0)
            qm = jnp.where(same_head, _dot(bht[p], only(tg[p])) + _dot(kht[p], only(v[:, sls[p]])), 0.0)
            ys[p].append(_dot(cy[p][rows], bf(hs[p])) + yg[p][rows])
            hs[p] = _dot3(*_split2(pm), *_split2(hs[p])) + qm
    for p in range(PAIRS):
        h_ref[p] = hs[p]
        y_ref[:, sls[p]] = jnp.concatenate(ys[p], axis=0)

    @pl.when(grp == pl.num_programs(1) - 1)
    def _():
        for p in range(PAIRS):
            st = h_ref[p].T
            sout_ref[0, 2 * p] = st[:DH_B, :DH_B]
            sout_ref[0, 2 * p + 1] = st[DH_B:, DH_B:]


def _rwkv_scan(vecs, n, t):
    idx = jnp.arange(GROUP)
    same = (idx[:, None] // CHUNK) == (idx[None, :] // CHUNK)
    tri = (same & (idx[None, :] <= idx[:, None])).astype(BF16)
    blk_ones = same.astype(BF16)
    groups = t // GROUP
    blk = pl.BlockSpec((GROUP, D_B), lambda s, g: (s * groups + g, 0))
    const = pl.BlockSpec((GROUP, GROUP), lambda s, g: (0, 0))
    return pl.pallas_call(
        _scan_kernel,
        grid=(n, groups),
        in_specs=[blk] * 7 + [const, const],
        out_specs=[blk, pl.BlockSpec((1, H_B, DH_B, DH_B), lambda s, g: (s, 0, 0, 0))],
        out_shape=[jax.ShapeDtypeStruct((n * t, D_B), F32),
                   jax.ShapeDtypeStruct((n, H_B, DH_B, DH_B), F32)],
        scratch_shapes=[pltpu.VMEM((PAIRS, LANES, LANES), F32)],
        compiler_params=_params(("parallel", "arbitrary")),
        name="rwkv_scan",
    )(*vecs, tri, blk_ones)


def _wkv_step_kernel(s_ref, kk_ref, w_ref, b_ref, k_ref, v_ref, c_ref, kr_ref, y_ref, so_ref, t_ref):
    hd = pl.program_id(0)

    @pl.when(hd == 0)
    def _():
        for i, ref in enumerate((kk_ref, w_ref, b_ref, k_ref, v_ref, c_ref, kr_ref)):
            t_ref[i] = ref[...].T

    base = pl.multiple_of(hd * DH_B, DH_B)
    kk, w, b, k, _, c, kr = (t_ref[i, pl.ds(base, DH_B), :] for i in range(7))
    kr_row = kr[0:1, :]

    def group(gi, carry):
        v0 = pl.multiple_of(gi * SUBLANES, SUBLANES)
        v8 = t_ref[4, pl.ds(base + v0, SUBLANES), :]
        ys = []
        for j in range(SUBLANES):
            s = s_ref[v0 + j]
            v_row = v8[j:j + 1, :]
            sa = -jnp.sum(s * kk, axis=0, keepdims=True)
            ys.append(jnp.sum(s * c, axis=0, keepdims=True) + v_row * kr_row)
            so_ref[v0 + j] = s * w + sa * b + v_row * k
        y_ref[pl.ds(v0, SUBLANES), :] = jnp.concatenate(ys, axis=0)
        return carry

    lax.fori_loop(0, DH_B // SUBLANES, group, 0)


def _wkv_step(state, vecs):
    nb = state.shape[0]
    sblk = pl.BlockSpec((None, DH_B, DH_B, nb), lambda h: (h, 0, 0, 0))
    vblk = pl.BlockSpec((nb, D_B), lambda h: (0, 0))
    y_t, s_t = pl.pallas_call(
        _wkv_step_kernel,
        grid=(H_B,),
        in_specs=[sblk] + [vblk] * 7,
        out_specs=[pl.BlockSpec((DH_B, nb), lambda h: (h, 0)), sblk],
        out_shape=[jax.ShapeDtypeStruct((D_B, nb), F32),
                   jax.ShapeDtypeStruct((H_B, DH_B, DH_B, nb), F32)],
        scratch_shapes=[pltpu.VMEM((7, D_B, nb), F32)],
        compiler_params=_params(("arbitrary",)),
        name="wkv_step",
    )(jnp.transpose(state, (1, 2, 3, 0)), *vecs)
    return y_t.T, jnp.transpose(s_t, (3, 0, 1, 2))


def _mix_ffn_kernel(ya_ref, y_ref, bv_ref, g_ref, lnw_ref, lnb_ref, ones_ref, h_ref, wo_ref, gmix_ref,
                    gpre_ref, gpost_ref, wg_ref, wu_ref, wd_ref, o_ref):
    ones = ones_ref[...]
    y = y_ref[...]
    mean = _segsum(y, ones) * (1.0 / DH_B)
    d = y - mean
    var = _segsum(d * d, ones) * (1.0 / DH_B)
    yn = d * lax.rsqrt(var + GN_EPS) * lnw_ref[...] + lnb_ref[...]
    yb = ((yn + bv_ref[...]) * g_ref[...]).astype(BF16)
    mix = _dot(ya_ref[...], wo_ref[0:D_A, :]) + _dot(yb, wo_ref[D_A:, :])
    h2 = h_ref[...] + _rms(mix, gmix_ref[...])
    o_ref[...] = _ffn_half_step(h2, gpre_ref, gpost_ref, wg_ref, wu_ref, wd_ref)


def _mix_ffn(ya, y, bv, g, ln_w, ln_b, ones, h, w_out, g_mix, g_pre, g_post, wg, wu, wd, tm):
    m = h.shape[0]
    row = lambda w: pl.BlockSpec((tm, w), lambda i: (i, 0))
    resident = lambda a: pl.BlockSpec(a.shape, lambda i: (0,) * a.ndim, pipeline_mode=pl.Buffered(1))
    consts = (ln_w, ln_b, ones)
    weights = (w_out, g_mix, g_pre, g_post, wg, wu, wd)
    return pl.pallas_call(
        _mix_ffn_kernel,
        grid=(m // tm,),
        in_specs=[row(D_A), row(D_B), row(D_B), row(D_B)] + [resident(a) for a in consts]
        + [row(D_MODEL)] + [resident(a) for a in weights],
        out_specs=row(D_MODEL),
        out_shape=jax.ShapeDtypeStruct((m, D_MODEL), F32),
        compiler_params=_params(("parallel",)),
        name="mix_ffn",
    )(ya, y, bv, g, *consts, h, *weights)


def _block_ones(n, seg):
    i = jnp.arange(n) // seg
    return (i[:, None] == i[None, :]).astype(BF16)


def kernel(x_prompt, x_sample, cache_k, cache_v, state_wkv, state_shift, page_table, n_ffn1_pre, n_ffn1_post, ffn1_gate, ffn1_up, ffn1_down, n_mix_pre, n_mix_post, w_in, w_out, lambda_q1, lambda_k1, lambda_q2, lambda_k2, subln, mu_shift, w0, w2, a0, a2, g2, k_k, k_a, r_k, ln_x_w, ln_x_b, n_ffn2_pre, n_ffn2_post, ffn2_gate, ffn2_up, ffn2_down):
    n_p, t_p, _ = x_prompt.shape
    n_s, t_s, _ = x_sample.shape
    assert t_s == 1
    depth = w_in.shape[0]
    n_pages = page_table.shape[1]
    past_len = n_pages * PAGE_SIZE
    ones_seg = _block_ones(D_B, DH_B)
    tab_p = _rope_tables(jnp.arange(t_p, dtype=jnp.int32))
    tab_s = _rope_tables(jnp.full((n_s,), past_len, jnp.int32))
    zeros_lora = jnp.zeros((LORA_W, D_B), F32)
    tm_p = 512
    tm_proj = 512

    yp = x_prompt.reshape(n_p * t_p, D_MODEL)
    ys = x_sample.reshape(n_s, D_MODEL)
    outs = [[] for _ in range(8)]
    for l in range(depth):
        lam_init = 0.8 - 0.6 * math.exp(-0.3 * l)
        vec = lambda a: a[l].reshape(1, -1)
        ffn1 = (vec(n_ffn1_pre), vec(n_ffn1_post), ffn1_gate[l].astype(BF16), ffn1_up[l].astype(BF16),
                ffn1_down[l].astype(BF16))
        ffn2 = (vec(n_ffn2_pre), vec(n_ffn2_post), ffn2_gate[l].astype(BF16), ffn2_up[l].astype(BF16),
                ffn2_down[l].astype(BF16))
        w_in_b = w_in[l].astype(BF16)
        w_out_b = w_out[l].astype(BF16)
        lams = (vec(lambda_q1), vec(lambda_k1), vec(lambda_q2), vec(lambda_k2))
        prep_w = (vec(mu_shift), vec(w0), vec(a0),
                  jnp.concatenate([w2[l], zeros_lora], axis=0).astype(BF16),
                  jnp.concatenate([zeros_lora, a2[l]], axis=0).astype(BF16),
                  g2[l].astype(BF16), vec(k_k), vec(k_a), r_k[l].reshape(1, D_B), ones_seg)

        def mix_tail(h, ya, y, bv, g, tm):
            return _mix_ffn(ya, y, bv, g, vec(ln_x_w), vec(ln_x_b), ones_seg, h, w_out_b, vec(n_mix_post),
                            *ffn2, tm)

        h = _ffn(yp, *ffn1, tm_p)
        qt, k, v, kb, vt, pb = _proj(h, vec(n_mix_pre), w_in_b, tab_p, tm_proj, t_p // tm_proj)
        ya = _attn_prompt(qt, kb, vt, lams, vec(subln), n_p, t_p, lam_init)
        prev0 = jnp.zeros((n_p, 1, SHIFT_DIM), F32)
        kk_, lw_, b_, k2_, v_, c_, kr_, bv_, g_ = _rwkv_prep(pb, prev0, prep_w, tm_proj, t_p)
        y, s_new = _rwkv_scan((lw_, kk_, b_, k2_, v_, c_, kr_), n_p, t_p)
        yp = mix_tail(h, ya, y, bv_, g_, tm_p)
        outs[0].append(k.reshape(n_p, t_p, H_A, 2 * DH_A))
        outs[1].append(v.reshape(n_p, t_p, H_A, 2 * DH_A))
        outs[2].append(s_new)
        outs[3].append(pb.reshape(n_p, t_p, SHIFT_DIM)[:, -1])

        h = _ffn(ys, *ffn1, n_s)
        qt, k, v, _, _, pb = _proj(h, vec(n_mix_pre), w_in_b, tab_s, n_s, 1)
        ya = _attn_decode(qt.T, k, v, cache_k, cache_v, l, page_table, lams, vec(subln), lam_init)
        kk_, w_, b_, k2_, v_, c_, kr_, bv_, g_ = _rwkv_prep(pb, state_shift[l], prep_w, n_s, 1)
        y, s_new = _wkv_step(state_wkv[l], (kk_, w_, b_, k2_, v_, c_, kr_))
        ys = mix_tail(h, ya, y, bv_, g_, n_s)
        outs[4].append(k.reshape(n_s, 1, H_A, 2 * DH_A))
        outs[5].append(v.reshape(n_s, 1, H_A, 2 * DH_A))
        outs[6].append(s_new)
        outs[7].append(pb)

    return (yp.reshape(n_p, t_p, D_MODEL), ys.reshape(n_s, 1, D_MODEL),
            *[jnp.stack(o) for o in outs])
```

```python
import functools
import math

import jax
import jax.numpy as jnp
from jax import lax
from jax.experimental import pallas as pl
from jax.experimental.pallas import tpu as pltpu

F32 = jnp.float32
BF16 = jnp.bfloat16

D_MODEL = 1024
H_A = 4
DH_A = 64
D_A = H_A * 2 * DH_A
ROT_DIM = DH_A // 4
ROPE_THETA = 500000.0
H_B = 8
DH_B = 64
D_B = H_B * DH_B
LORA_W = 64
LORA_A = 64
LORA_G = 128
SHIFT_DIM = 3 * D_B + LORA_W + LORA_A + LORA_G
D_IN = 3 * D_A + SHIFT_DIM
D_FF = 2816
PAGE_SIZE = 128
NORM_EPS = 1e-6
GN_EPS = 64e-5

LANES = 128
SUBLANES = 8
VMEM_LIMIT = 48 * 1024 * 1024


def _dot(a, b):
    return jnp.dot(a, b, preferred_element_type=F32)


def _dot_nt(a, b):
    return lax.dot_general(a, b, (((1,), (1,)), ((), ())), preferred_element_type=F32)


def _rms(x, g):
    return x * lax.rsqrt(jnp.mean(x * x, axis=-1, keepdims=True) + NORM_EPS) * g


def _sigmoid(x):
    return 1.0 / (1.0 + jnp.exp(-x))


def _split2(x):
    hi = x.astype(BF16)
    mid = (x - hi.astype(F32)).astype(BF16)
    return hi, mid


def _segsum(x, ones):
    hi, mid = _split2(x)
    return _dot(hi, ones) + _dot(mid, ones)


def _params(sem):
    return pltpu.CompilerParams(dimension_semantics=sem, vmem_limit_bytes=VMEM_LIMIT)


def _ffn_half_step(x, gpre_ref, gpost_ref, wg_ref, wu_ref, wd_ref):
    un = _rms(x, gpre_ref[...]).astype(BF16)
    g = _dot(un, wg_ref[...])
    u = _dot(un, wu_ref[...])
    hid = ((g * _sigmoid(g)) * u).astype(BF16)
    return x + 0.5 * _rms(_dot(hid, wd_ref[...]), gpost_ref[...])


def _ffn_kernel(x_ref, gpre_ref, gpost_ref, wg_ref, wu_ref, wd_ref, o_ref):
    o_ref[...] = _ffn_half_step(x_ref[...], gpre_ref, gpost_ref, wg_ref, wu_ref, wd_ref)


def _ffn(x, g_pre, g_post, wg, wu, wd, tm):
    m = x.shape[0]
    row = pl.BlockSpec((tm, D_MODEL), lambda i: (i, 0))
    vec = pl.BlockSpec((1, D_MODEL), lambda i: (0, 0))
    resident = lambda w: pl.BlockSpec(w.shape, lambda i: (0, 0), pipeline_mode=pl.Buffered(1))
    return pl.pallas_call(
        _ffn_kernel,
        grid=(m // tm,),
        in_specs=[row, vec, vec, resident(wg), resident(wu), resident(wd)],
        out_specs=row,
        out_shape=jax.ShapeDtypeStruct((m, D_MODEL), F32),
        compiler_params=_params(("parallel",)),
        name="ffn",
    )(x, g_pre, g_post, wg, wu, wd)


def _proj_kernel(h_ref, g_ref, w_ref, cos_ref, sina_ref, sinb_ref,
                 qt_ref, k_ref, v_ref, kb_ref, vt_ref, pb_ref):
    u = _rms(h_ref[...], g_ref[...]).astype(BF16)
    cos = cos_ref[...]
    sina = sina_ref[...]
    sinb = sinb_ref[...]
    half = ROT_DIM // 2

    def rope(x):
        return x * cos + pltpu.roll(x, LANES - half, 1) * sina + pltpu.roll(x, half, 1) * sinb

    qa = _dot(u, w_ref[:, 0:D_A])
    ka = _dot(u, w_ref[:, D_A:2 * D_A])
    for hh in range(H_A):
        sl = slice(hh * LANES, (hh + 1) * LANES)
        qt_ref[sl, :] = (rope(qa[:, sl]) * (DH_A ** -0.5)).T.astype(BF16)
        kh = rope(ka[:, sl])
        k_ref[:, hh, :] = kh
        kb_ref[:, sl] = kh.astype(BF16)
    va = _dot(u, w_ref[:, 2 * D_A:3 * D_A])
    for hh in range(H_A):
        v_ref[:, hh, :] = va[:, hh * LANES:(hh + 1) * LANES]
    vt_ref[...] = va.T.astype(BF16)
    pb_ref[...] = _dot(u, w_ref[:, 3 * D_A:])


def _proj(h, g, w_in, tables, tm, table_blocks):
    m = h.shape[0]
    row = lambda width: pl.BlockSpec((tm, width), lambda i: (i, 0))
    tab = pl.BlockSpec((tm, LANES), lambda i: (i % table_blocks, 0))
    per_head = pl.BlockSpec((tm, H_A, 2 * DH_A), lambda i: (i, 0, 0))
    cols = pl.BlockSpec((D_A, tm), lambda i: (0, i))
    shp = lambda width, dt: jax.ShapeDtypeStruct((m, width), dt)
    return pl.pallas_call(
        _proj_kernel,
        grid=(m // tm,),
        in_specs=[row(D_MODEL), pl.BlockSpec((1, D_MODEL), lambda i: (0, 0)),
                  pl.BlockSpec((D_MODEL, D_IN), lambda i: (0, 0)), tab, tab, tab],
        out_specs=[cols, per_head, per_head, row(D_A), cols, row(SHIFT_DIM)],
        out_shape=[jax.ShapeDtypeStruct((D_A, m), BF16), jax.ShapeDtypeStruct((m, H_A, 2 * DH_A), F32),
                   jax.ShapeDtypeStruct((m, H_A, 2 * DH_A), F32), shp(D_A, BF16),
                   jax.ShapeDtypeStruct((D_A, m), BF16), shp(SHIFT_DIM, F32)],
        compiler_params=_params(("parallel",)),
        name="proj",
    )(h, g, w_in, *tables)


def _rope_tables(pos):
    half = ROT_DIM // 2
    inv_freq = ROPE_THETA ** (-jnp.arange(half, dtype=F32) / half)
    rest = jnp.zeros((DH_A - ROT_DIM,), F32)
    per_lane = jnp.concatenate([inv_freq, inv_freq, rest] * 2)
    d = jnp.arange(LANES) % DH_A
    ang = pos.astype(F32)[:, None] * per_lane[None, :]
    sin = jnp.sin(ang)
    return (jnp.cos(ang), jnp.where(d < half, -sin, 0.0),
            jnp.where((d >= half) & (d < ROT_DIM), sin, 0.0))


def _lambda(lq1_ref, lk1_ref, lq2_ref, lk2_ref, lam_init):
    s1 = jnp.sum(lq1_ref[...] * lk1_ref[...], axis=-1, keepdims=True)
    s2 = jnp.sum(lq2_ref[...] * lk2_ref[...], axis=-1, keepdims=True)
    return jnp.exp(s1) - jnp.exp(s2) + lam_init


def _attn_kernel(qt_ref, k_ref, vt_ref, lq1_ref, lk1_ref, lq2_ref, lk2_ref, subln_ref, o_ref,
                 m_ref, acc_ref, sta_ref, stb_ref, *, tq, lam_init):
    krow = lax.broadcasted_iota(jnp.int32, (tq, tq), 0)
    qcol = lax.broadcasted_iota(jnp.int32, (tq, tq), 1)
    dim = lax.broadcasted_iota(jnp.int32, (LANES, tq), 0)
    ones_rows = jnp.ones((acc_ref.shape[1] - LANES, tq), BF16)
    lam = _lambda(lq1_ref, lk1_ref, lq2_ref, lk2_ref, lam_init)
    maps = range(2)
    start = lambda i: pl.multiple_of(i * tq, tq)

    def score(qs, kstart, st_ref):
        k = k_ref[pl.ds(kstart, tq), :]
        for j in maps:
            st_ref[j] = _dot(k, qs[j])

    def consume(kstart, st_ref, diagonal):
        vt = jnp.concatenate([vt_ref[:, pl.ds(kstart, tq)], ones_rows], axis=0)
        st = [st_ref[j] for j in maps]
        if diagonal:
            st = [jnp.where(krow <= qcol, s, -jnp.inf) for s in st]
        m_prev = [m_ref[j] for j in maps]
        m_new = [jnp.maximum(m_prev[j], jnp.max(st[j], axis=0, keepdims=True)) for j in maps]
        p = [jnp.exp(st[j] - m_new[j]) for j in maps]
        alpha = [jnp.exp(m_prev[j] - m_new[j]) for j in maps]
        pv = [_dot(vt, p[j].astype(BF16)) for j in maps]
        for j in maps:
            acc_ref[j] = alpha[j] * acc_ref[j] + pv[j]
            m_ref[j] = m_new[j]

    def tile(qi, carry):
        m_ref[...] = jnp.full(m_ref.shape, -jnp.inf, F32)
        acc_ref[...] = jnp.zeros_like(acc_ref)
        qt = qt_ref[:, pl.ds(start(qi), tq)].astype(F32)
        qs = (jnp.where(dim < DH_A, qt, 0.0).astype(BF16), jnp.where(dim >= DH_A, qt, 0.0).astype(BF16))
        score(qs, 0, sta_ref)

        def body(ki, c):
            for parity, (cur, nxt) in enumerate(((sta_ref, stb_ref), (stb_ref, sta_ref))):
                @pl.when(ki % 2 == parity)
                def _(cur=cur, nxt=nxt):
                    score(qs, start(ki + 1), nxt)
                    consume(start(ki), cur, False)
            return c

        lax.fori_loop(0, qi, body, 0)
        for parity, cur in enumerate((sta_ref, stb_ref)):
            @pl.when(qi % 2 == parity)
            def _(cur=cur):
                consume(start(qi), cur, True)

        norm = lambda j: acc_ref[j, 0:LANES, :] / acc_ref[j, LANES:LANES + 1, :]
        ot = norm(0) - lam * norm(1)
        o_ref[pl.ds(start(qi), tq), :] = (_rms(ot.T, subln_ref[...]) * (1.0 - lam_init)).astype(BF16)
        return carry

    lax.fori_loop(0, qt_ref.shape[1] // tq, tile, 0)


def _attn_prompt(qt, kb, vt, lams, subln, n, t, lam_init):
    tq = 512
    rows = pl.BlockSpec((t, LANES), lambda b, h: (b, h))
    cols = pl.BlockSpec((LANES, t), lambda b, h: (h, b))
    small = lambda w: pl.BlockSpec((1, w), lambda b, h: (0, 0))
    return pl.pallas_call(
        functools.partial(_attn_kernel, tq=tq, lam_init=lam_init),
        grid=(n, H_A),
        in_specs=[cols, rows, cols, small(DH_A), small(DH_A), small(DH_A), small(DH_A), small(2 * DH_A)],
        out_specs=rows,
        out_shape=jax.ShapeDtypeStruct((n * t, D_A), BF16),
        scratch_shapes=[pltpu.VMEM((2, 1, tq), F32), pltpu.VMEM((2, LANES + 16, tq), F32),
                        pltpu.VMEM((2, tq, tq), F32), pltpu.VMEM((2, tq, tq), F32)],
        compiler_params=_params(("parallel", "parallel")),
        name="attn_prompt",
    )(qt, kb, vt, *lams, subln)


def _attn_decode_kernel(pt_ref, q_ref, ks_ref, vs_ref, lq1_ref, lk1_ref, lq2_ref, lk2_ref, subln_ref,
                        *rest, n_pages, lam_init):
    del pt_ref
    kp_refs = rest[:n_pages]
    vp_refs = rest[n_pages:2 * n_pages]
    o_ref = rest[2 * n_pages]
    nmap = 2 * H_A
    page_rows = PAGE_SIZE * H_A
    heads = lambda x: jnp.concatenate([x[:, hh * LANES:(hh + 1) * LANES] for hh in range(H_A)], axis=0)
    q4 = heads(q_ref[0].astype(F32))
    k4 = heads(ks_ref[0].astype(BF16).astype(F32))
    v4 = heads(vs_ref[0].astype(BF16).astype(F32))
    r8 = lax.broadcasted_iota(jnp.int32, (nmap, LANES), 0)
    l8 = lax.broadcasted_iota(jnp.int32, (nmap, LANES), 1)
    q8 = jnp.where((l8 // DH_A) == (r8 // H_A), jnp.concatenate([q4, q4], axis=0), 0.0)
    q8_b = q8.astype(BF16)
    s = jnp.concatenate([_dot_nt(q8_b, kp_refs[pg][...].astype(BF16)) for pg in range(n_pages)], axis=1)
    rs = lax.broadcasted_iota(jnp.int32, s.shape, 0)
    cs = lax.broadcasted_iota(jnp.int32, s.shape, 1)
    s = jnp.where((cs % H_A) == (rs % H_A), s, -jnp.inf)
    s_self = jnp.sum(q8 * jnp.concatenate([k4, k4], axis=0), axis=1, keepdims=True)
    m = jnp.maximum(jnp.max(s, axis=1, keepdims=True), s_self)
    e = jnp.exp(s - m)
    e_self = jnp.exp(s_self - m)
    inv = 1.0 / (jnp.sum(e, axis=1, keepdims=True) + e_self)
    lam = _lambda(lq1_ref, lk1_ref, lq2_ref, lk2_ref, lam_init)
    p = e * inv
    p_self = e_self * inv
    pc = (p[0:H_A] - lam * p[H_A:nmap]).astype(BF16)
    pc_self = (p_self[0:H_A] - lam * p_self[H_A:nmap]).astype(BF16).astype(F32)
    o = pc_self * v4
    for pg in range(n_pages):
        o = o + _dot(pc[:, pg * page_rows:(pg + 1) * page_rows], vp_refs[pg][...].astype(BF16))
    o = (_rms(o, subln_ref[...]) * (1.0 - lam_init)).astype(BF16)
    for hh in range(H_A):
        o_ref[0, :, hh * LANES:(hh + 1) * LANES] = o[hh:hh + 1, :]


def _attn_decode(q, k_self, v_self, cache_k, cache_v, layer, page_table, lams, subln, lam_init):
    nb, n_pages = page_table.shape
    tok = pl.BlockSpec((1, 1, D_A), lambda b, pt: (b, 0, 0))
    small = lambda w: pl.BlockSpec((1, w), lambda b, pt: (0, 0))
    as_rows = lambda c: c.reshape(c.shape[0], c.shape[1], PAGE_SIZE * H_A, 2 * DH_A)
    cache_k, cache_v = as_rows(cache_k), as_rows(cache_v)
    page = lambda p: pl.BlockSpec((None, None, PAGE_SIZE * H_A, 2 * DH_A),
                                  lambda b, pt: (layer, pt[b, p], 0, 0))
    pages = [page(p) for p in range(n_pages)]
    grid_spec = pltpu.PrefetchScalarGridSpec(
        num_scalar_prefetch=1,
        grid=(nb,),
        in_specs=[tok, tok, tok, small(DH_A), small(DH_A), small(DH_A), small(DH_A), small(2 * DH_A)]
        + pages + pages,
        out_specs=tok,
    )
    out = pl.pallas_call(
        functools.partial(_attn_decode_kernel, n_pages=n_pages, lam_init=lam_init),
        grid_spec=grid_spec,
        out_shape=jax.ShapeDtypeStruct((nb, 1, D_A), BF16),
        compiler_params=_params(("arbitrary",)),
        name="attn_decode",
    )(page_table, q.reshape(nb, 1, D_A), k_self.reshape(nb, 1, D_A), v_self.reshape(nb, 1, D_A),
      *lams, subln, *([cache_k] * n_pages), *([cache_v] * n_pages))
    return out.reshape(nb, D_A)


def _prep_kernel(*refs, tm, tiles_per_seq):
    if tiles_per_seq:
        pb_ref, prev_ref, tail_ref = refs[:3]
        refs = refs[3:]
    else:
        pb_ref, prev_ref = refs[:2]
        refs = refs[2:]
    (mu_ref, w0_ref, a0_ref, w2_ref, a2_ref, g2_ref, kk_w_ref, ka_w_ref, rk_w_ref, ones_ref,
     kk_o, w_o, b_o, k_o, v_o, c_o, kr_o, bv_o, g_o) = refs
    pb = pb_ref[...]
    if tiles_per_seq:
        first = (pl.program_id(0) % tiles_per_seq) == 0
        prev_row = jnp.where(first, prev_ref[0], tail_ref[SUBLANES - 1:SUBLANES, :])
        rows = lax.broadcasted_iota(jnp.int32, pb.shape, 0)
        shifted = jnp.where(rows == 0, jnp.broadcast_to(prev_row, pb.shape), pltpu.roll(pb, 1, 0))
    else:
        shifted = prev_ref[...]
    xs = pb + (shifted - pb) * mu_ref[...]
    r = xs[:, 0:D_B]
    k = xs[:, D_B:2 * D_B]
    v = xs[:, 2 * D_B:3 * D_B]
    xwa = xs[:, 3 * D_B:3 * D_B + LORA_W + LORA_A]
    xg = xs[:, 3 * D_B + LORA_W + LORA_A:]
    ones = ones_ref[...]
    w_raw = w0_ref[...] + _dot(jnp.tanh(xwa).astype(BF16), w2_ref[...])
    z = -w_raw
    softplus = jnp.maximum(z, 0.0) + jnp.log(1.0 + jnp.exp(-jnp.abs(z)))
    log_decay = -jnp.exp(-softplus - 0.5)
    decay = jnp.exp(log_decay)
    a =_sigmoid(a0_ref[...] + _dot(xwa.astype(BF16), a2_ref[...]))
    g = _dot(_sigmoid(xg).astype(BF16), g2_ref[...])
    kk = k * kk_w_ref[...]
    kk = kk / jnp.maximum(jnp.sqrt(_segsum(kk * kk, ones)), 1e-12)
    k2 = k * (1.0 + (a - 1.0) * ka_w_ref[...])
    b = kk * a
    br = _segsum(b * r, ones)
    kr = _segsum(k2 * r, ones)
    bonus = _segsum(r * k2 * rk_w_ref[...], ones)
    kk_o[...] = kk
    w_o[...] = log_decay if tiles_per_seq else decay
    b_o[...] = b
    k_o[...] = k2
    v_o[...] = v
    c_o[...] = decay * r - kk * br
    kr_o[...] = kr
    bv_o[...] = bonus * v
    g_o[...] = g


def _rwkv_prep(pb, prev, weights, tm, seq_len):
    m = pb.shape[0]
    row = lambda w: pl.BlockSpec((tm, w), lambda i: (i, 0))
    const = lambda a: pl.BlockSpec(a.shape, lambda i: (0,) * a.ndim)
    if seq_len > 1:
        tiles_per_seq = seq_len // tm
        tail = pl.BlockSpec((SUBLANES, SHIFT_DIM),
                            lambda i: (jnp.maximum(i * (tm // SUBLANES) - 1, 0), 0))
        head = [row(SHIFT_DIM), pl.BlockSpec((1, 1, SHIFT_DIM), lambda i: (i // tiles_per_seq, 0, 0)), tail]
        args = [pb, prev, pb]
    else:
        tiles_per_seq = 0
        head = [row(SHIFT_DIM), row(SHIFT_DIM)]
        args = [pb, prev]
    return pl.pallas_call(
        functools.partial(_prep_kernel, tm=tm, tiles_per_seq=tiles_per_seq),
        grid=(m // tm,),
        in_specs=head + [const(a) for a in weights],
        out_specs=[row(D_B)] * 9,
        out_shape=[jax.ShapeDtypeStruct((m, D_B), F32)] * 9,
        compiler_params=_params(("parallel",)),
        name="rwkv_prep",
    )(*args, *weights)


PAIRS = H_B // 2
CHUNK = 64
GROUP = 4 * CHUNK


def _dot3(a_hi, a_mid, b_hi, b_mid):
    return _dot(a_hi, b_hi) + _dot(a_hi, b_mid) + _dot(a_mid, b_hi)


def _scan_kernel(lw_ref, kk_ref, b_ref, k_ref, v_ref, c_ref, kr_ref, tri_ref, blk_ref,
                 y_ref, sout_ref, h_ref):
    grp = pl.program_id(1)

    @pl.when(grp == 0)
    def _():
        h_ref[...] = jnp.zeros_like(h_ref)

    n = GROUP
    tri = tri_ref[...]
    blk = blk_ref[...]
    lw = lw_ref[...]
    lw_hi = lw.astype(BF16)
    lw_r = lw - lw_hi.astype(F32)
    lw_mid = lw_r.astype(BF16)
    lw_lo = (lw_r - lw_mid.astype(F32)).astype(BF16)
    cum = _dot(tri, lw_hi) + _dot(tri, lw_mid) + _dot(tri, lw_lo)
    tot = _dot(blk, lw_hi) + _dot(blk, lw_mid) + _dot(blk, lw_lo)
    g_prev = jnp.exp(cum - lw)
    g_inv = jnp.exp(-cum)
    g_end = jnp.exp(tot - cum)
    g_tot = jnp.exp(tot)
    kk = kk_ref[...]
    b = b_ref[...]
    k = k_ref[...]
    v = v_ref[...]
    at = -(kk * g_prev)
    ct = c_ref[...] * g_prev
    bt = b * g_inv
    kt = k * g_inv
    bh = b * g_end
    kh = k * g_end
    krv = kr_ref[...] * v

    ri = lax.broadcasted_iota(jnp.int32, (n, n), 0)
    ci = lax.broadcasted_iota(jnp.int32, (n, n), 1)
    first = (ri // CHUNK) * CHUNK
    stril = ((ci - first) | (ri - 1 - ci)) >= 0
    eye_n = (ri == ci).astype(F32)
    lane = lax.broadcasted_iota(jnp.int32, (n, LANES), 1)
    rown = lax.broadcasted_iota(jnp.int32, (n, LANES), 0)
    head0 = lane < DH_B
    r2 = lax.broadcasted_iota(jnp.int32, (LANES, LANES), 0)
    c2 = lax.broadcasted_iota(jnp.int32, (LANES, LANES), 1)
    same_head = (r2 // DH_B) == (c2 // DH_B)
    eye_l = r2 == c2

    heads = [(p, j) for p in range(PAIRS) for j in range(2)]
    sls = [slice(p * LANES, (p + 1) * LANES) for p in range(PAIRS)]
    bf = lambda x: x.astype(BF16)
    v_b = [bf(v[:, sl]) for sl in sls]
    bk = [bf(jnp.concatenate([bt[:, sl].T, kt[:, sl].T], axis=1)) for sl in sls]
    xs = []
    for p, j in heads:
        mine = head0 if j == 0 else lane >= DH_B
        lhs = jnp.concatenate([jnp.where(mine, at[:, sls[p]], 0.0), jnp.where(mine, ct[:, sls[p]], 0.0)], axis=0)
        xs.append(_dot(bf(lhs), bk[p]))
    lab = [jnp.where(stril, x[:n, :n], 0.0) for x in xs]
    lak = [jnp.where(stril, x[:n, n:], 0.0) for x in xs]
    mcb = [jnp.where(stril, x[n:, :n], 0.0) for x in xs]
    mck = [jnp.where(stril, x[n:, n:], 0.0) for x in xs]
    tm = [eye_n + x for x in lab]
    xb = [bf(x) for x in lab]
    xb = [bf(_dot(x, x)) for x in xb]
    for _ in range(4):
        both = [_dot(jnp.concatenate([bf(t), x], axis=0), x) for t, x in zip(tm, xb)]
        tm = [t + r[:n] for t, r in zip(tm, both)]
        xb = [bf(r[n:]) for r in both]
    tm = [t + _dot(bf(t), x) for t, x in zip(tm, xb)]
    gm = [_dot(bf(lak[i]), v_b[p]) for i, (p, j) in enumerate(heads)]
    tag = [_dot(bf(tm[i]), bf(jnp.concatenate([at[:, sls[p]], gm[i]], axis=1)))
           for i, (p, j) in enumerate(heads)]
    mt = [_dot(bf(mcb[i]), bf(tag[i])) for i in range(len(heads))]
    mv = [_dot(bf(mck[i]), v_b[p]) for i, (p, j) in enumerate(heads)]

    ta, tg, cy, yg, bht, kht, hs = [], [], [], [], [], [], []
    for p in range(PAIRS):
        pick = lambda f: jnp.where(head0, f(2 * p), f(2 * p + 1))
        ta.append(pick(lambda i: tag[i][:, :LANES]))
        tg.append(pick(lambda i: tag[i][:, LANES:]))
        cy.append(bf(ct[:, sls[p]] + pick(lambda i: mt[i][:, :LANES])))
        yg.append(pick(lambda i: mt[i][:, LANES:] + mv[i]) + krv[:, sls[p]])
        bht.append(bf(bh[:, sls[p]].T))
        kht.append(bf(kh[:, sls[p]].T))
        hs.append(h_ref[p])
    ys = [[] for _ in range(PAIRS)]
    for cidx in range(GROUP // CHUNK):
        rows = slice(cidx * CHUNK, (cidx + 1) * CHUNK)
        in_chunk = (rown // CHUNK) == cidx
        for p in range(PAIRS):
            only = lambda x: bf(jnp.where(in_chunk, x, 0.0))
            decay_c = jnp.broadcast_to(g_tot[cidx * CHUNK:cidx * CHUNK + 1, sls[p]], (LANES, LANES))
            pm = jnp.where(eye_l, decay_c, 0.0) + jnp.where(same_head, _dot(bht[p], only(ta[p])), 0.0)
            qm = jnp.where(same_head, _dot(bht[p], only(tg[p])) + _dot(kht[p], only(v[:, sls[p]])), 0.0)
            ys[p].append(_dot(cy[p][rows], bf(hs[p])) + yg[p][rows])
            hs[p] = _dot3(*_split2(pm), *_split2(hs[p])) + qm
    for p in range(PAIRS):
        h_ref[p] = hs[p]
        y_ref[:, sls[p]] = jnp.concatenate(ys[p], axis=0)

    @pl.when(grp == pl.num_programs(1) - 1)
    def _():
        for p in range(PAIRS):
            st = h_ref[p].T
            sout_ref[0, 2 * p] = st[:DH_B, :DH_B]
            sout_ref[0, 2 * p + 1] = st[DH_B:, DH_B:]


def _rwkv_scan(vecs, n, t):
    idx = jnp.arange(GROUP)
    same = (idx[:, None] // CHUNK) == (idx[None, :] // CHUNK)
    tri = (same & (idx[None, :] <= idx[:, None])).astype(BF16)
    blk_ones = same.astype(BF16)
    groups = t // GROUP
    blk = pl.BlockSpec((GROUP, D_B), lambda s, g: (s * groups + g, 0))
    const = pl.BlockSpec((GROUP, GROUP), lambda s, g: (0, 0))
    return pl.pallas_call(
        _scan_kernel,
        grid=(n, groups),
        in_specs=[blk] * 7 + [const, const],
        out_specs=[blk, pl.BlockSpec((1, H_B, DH_B, DH_B), lambda s, g: (s, 0, 0, 0))],
        out_shape=[jax.ShapeDtypeStruct((n * t, D_B), F32),
                   jax.ShapeDtypeStruct((n, H_B, DH_B, DH_B), F32)],
        scratch_shapes=[pltpu.VMEM((PAIRS, LANES, LANES), F32)],
        compiler_params=_params(("parallel", "arbitrary")),
        name="rwkv_scan",
    )(*vecs, tri, blk_ones)


def _wkv_step_kernel(s_ref, kk_ref, w_ref, b_ref, k_ref, v_ref, c_ref, kr_ref, y_ref, so_ref, t_ref):
    hd = pl.program_id(0)

    @pl.when(hd == 0)
    def _():
        for i, ref in enumerate((kk_ref, w_ref, b_ref, k_ref, v_ref, c_ref, kr_ref)):
            t_ref[i] = ref[...].T

    base = pl.multiple_of(hd * DH_B, DH_B)
    kk, w, b, k, _, c, kr = (t_ref[i, pl.ds(base, DH_B), :] for i in range(7))
    kr_row = kr[0:1, :]

    def group(gi, carry):
        v0 = pl.multiple_of(gi * SUBLANES, SUBLANES)
        v8 = t_ref[4, pl.ds(base + v0, SUBLANES), :]
        ys = []
        for j in range(SUBLANES):
            s = s_ref[v0 + j]
            v_row = v8[j:j + 1, :]
            sa = -jnp.sum(s * kk, axis=0, keepdims=True)
            ys.append(jnp.sum(s * c, axis=0, keepdims=True) + v_row * kr_row)
            so_ref[v0 + j] = s * w + sa * b + v_row * k
        y_ref[pl.ds(v0, SUBLANES), :] = jnp.concatenate(ys, axis=0)
        return carry

    lax.fori_loop(0, DH_B // SUBLANES, group, 0)


def _wkv_step(state, vecs):
    nb = state.shape[0]
    sblk = pl.BlockSpec((None, DH_B, DH_B, nb), lambda h: (h, 0, 0, 0))
    vblk = pl.BlockSpec((nb, D_B), lambda h: (0, 0))
    y_t, s_t = pl.pallas_call(
        _wkv_step_kernel,
        grid=(H_B,),
        in_specs=[sblk] + [vblk] * 7,
        out_specs=[pl.BlockSpec((DH_B, nb), lambda h: (h, 0)), sblk],
        out_shape=[jax.ShapeDtypeStruct((D_B, nb), F32),
                   jax.ShapeDtypeStruct((H_B, DH_B, DH_B, nb), F32)],
        scratch_shapes=[pltpu.VMEM((7, D_B, nb), F32)],
        compiler_params=_params(("arbitrary",)),
        name="wkv_step",
    )(jnp.transpose(state, (1, 2, 3, 0)), *vecs)
    return y_t.T, jnp.transpose(s_t, (3, 0, 1, 2))


def _mix_ffn_kernel(ya_ref, y_ref, bv_ref, g_ref, lnw_ref, lnb_ref, ones_ref, h_ref, wo_ref, gmix_ref,
                    gpre_ref, gpost_ref, wg_ref, wu_ref, wd_ref, o_ref):
    ones = ones_ref[...]
    y = y_ref[...]
    mean = _segsum(y, ones) * (1.0 / DH_B)
    d = y - mean
    var = _segsum(d * d, ones) * (1.0 / DH_B)
    yn = d * lax.rsqrt(var + GN_EPS) * lnw_ref[...] + lnb_ref[...]
    yb = ((yn + bv_ref[...]) * g_ref[...]).astype(BF16)
    mix = _dot(ya_ref[...], wo_ref[0:D_A, :]) + _dot(yb, wo_ref[D_A:, :])
    h2 = h_ref[...] + _rms(mix, gmix_ref[...])
    o_ref[...] = _ffn_half_step(h2, gpre_ref, gpost_ref, wg_ref, wu_ref, wd_ref)


def _mix_ffn(ya, y, bv, g, ln_w, ln_b, ones, h, w_out, g_mix, g_pre, g_post, wg, wu, wd, tm):
    m = h.shape[0]
    row = lambda w: pl.BlockSpec((tm, w), lambda i: (i, 0))
    resident = lambda a: pl.BlockSpec(a.shape, lambda i: (0,) * a.ndim, pipeline_mode=pl.Buffered(1))
    consts = (ln_w, ln_b, ones)
    weights = (w_out, g_mix, g_pre, g_post, wg, wu, wd)
    return pl.pallas_call(
        _mix_ffn_kernel,
        grid=(m // tm,),
        in_specs=[row(D_A), row(D_B), row(D_B), row(D_B)] + [resident(a) for a in consts]
        + [row(D_MODEL)] + [resident(a) for a in weights],
        out_specs=row(D_MODEL),
        out_shape=jax.ShapeDtypeStruct((m, D_MODEL), F32),
        compiler_params=_params(("parallel",)),
        name="mix_ffn",
    )(ya, y, bv, g, *consts, h, *weights)


def _block_ones(n, seg):
    i = jnp.arange(n) // seg
    return (i[:, None] == i[None, :]).astype(BF16)


def kernel(x_prompt, x_sample, cache_k, cache_v, state_wkv, state_shift, page_table, n_ffn1_pre, n_ffn1_post, ffn1_gate, ffn1_up, ffn1_down, n_mix_pre, n_mix_post, w_in, w_out, lambda_q1, lambda_k1, lambda_q2, lambda_k2, subln, mu_shift, w0, w2, a0, a2, g2, k_k, k_a, r_k, ln_x_w, ln_x_b, n_ffn2_pre, n_ffn2_post, ffn2_gate, ffn2_up, ffn2_down):
    n_p, t_p, _ = x_prompt.shape
    n_s, t_s, _ = x_sample.shape
    assert t_s == 1
    depth = w_in.shape[0]
    n_pages = page_table.shape[1]
    past_len = n_pages * PAGE_SIZE
    ones_seg = _block_ones(D_B, DH_B)
    tab_p = _rope_tables(jnp.arange(t_p, dtype=jnp.int32))
    tab_s = _rope_tables(jnp.full((n_s,), past_len, jnp.int32))
    zeros_lora = jnp.zeros((LORA_W, D_B), F32)
    tm_p = 512
    tm_proj = 512

    yp = x_prompt.reshape(n_p * t_p, D_MODEL)
    ys = x_sample.reshape(n_s, D_MODEL)
    outs = [[] for _ in range(8)]
    for l in range(depth):
        lam_init = 0.8 - 0.6 * math.exp(-0.3 * l)
        vec = lambda a: a[l].reshape(1, -1)
        ffn1 = (vec(n_ffn1_pre), vec(n_ffn1_post), ffn1_gate[l].astype(BF16), ffn1_up[l].astype(BF16),
                ffn1_down[l].astype(BF16))
        ffn2 = (vec(n_ffn2_pre), vec(n_ffn2_post), ffn2_gate[l].astype(BF16), ffn2_up[l].astype(BF16),
                ffn2_down[l].astype(BF16))
        w_in_b = w_in[l].astype(BF16)
        w_out_b = w_out[l].astype(BF16)
        lams = (vec(lambda_q1), vec(lambda_k1), vec(lambda_q2), vec(lambda_k2))
        prep_w = (vec(mu_shift), vec(w0), vec(a0),
                  jnp.concatenate([w2[l], zeros_lora], axis=0).astype(BF16),
                  jnp.concatenate([zeros_lora, a2[l]], axis=0).astype(BF16),
                  g2[l].astype(BF16), vec(k_k), vec(k_a), r_k[l].reshape(1, D_B), ones_seg)

        def mix_tail(h, ya, y, bv, g, tm):
            return _mix_ffn(ya, y, bv, g, vec(ln_x_w), vec(ln_x_b), ones_seg, h, w_out_b, vec(n_mix_post),
                            *ffn2, tm)

        h = _ffn(yp, *ffn1, tm_p)
        qt, k, v, kb, vt, pb = _proj(h, vec(n_mix_pre), w_in_b, tab_p, tm_proj, t_p // tm_proj)
        ya = _attn_prompt(qt, kb, vt, lams, vec(subln), n_p, t_p, lam_init)
        prev0 = jnp.zeros((n_p, 1, SHIFT_DIM), F32)
        kk_, lw_, b_, k2_, v_, c_, kr_, bv_, g_ = _rwkv_prep(pb, prev0, prep_w, tm_proj, t_p)
        y, s_new = _rwkv_scan((lw_, kk_, b_, k2_, v_, c_, kr_), n_p, t_p)
        yp = mix_tail(h, ya, y, bv_, g_, tm_p)
        outs[0].append(k.reshape(n_p, t_p, H_A, 2 * DH_A))
        outs[1].append(v.reshape(n_p, t_p, H_A, 2 * DH_A))
        outs[2].append(s_new)
        outs[3].append(pb.reshape(n_p, t_p, SHIFT_DIM)[:, -1])

        h = _ffn(ys, *ffn1, n_s)
        qt, k, v, _, _, pb = _proj(h, vec(n_mix_pre), w_in_b, tab_s, n_s, 1)
        ya = _attn_decode(qt.T, k, v, cache_k, cache_v, l, page_table, lams, vec(subln), lam_init)
        kk_, w_, b_, k2_, v_, c_, kr_, bv_, g_ = _rwkv_prep(pb, state_shift[l], prep_w, n_s, 1)
        y, s_new = _wkv_step(state_wkv[l], (kk_, w_, b_, k2_, v_, c_, kr_))
        ys = mix_tail(h, ya, y, bv_, g_, n_s)
        outs[4].append(k.reshape(n_s, 1, H_A, 2 * DH_A))
        outs[5].append(v.reshape(n_s, 1, H_A, 2 * DH_A))
        outs[6].append(s_new)
        outs[7].append(pb)

    return (yp.reshape(n_p, t_p, D_MODEL), ys.reshape(n_s, 1, D_MODEL),
            *[jnp.stack(o) for o in outs])
```

```python
import functools
import math

import jax
import jax.numpy as jnp
from jax import lax
from jax.experimental import pallas as pl
from jax.experimental.pallas import tpu as pltpu

F32 = jnp.float32
BF16 = jnp.bfloat16

D_MODEL = 1024
H_A = 4
DH_A = 64
D_A = H_A * 2 * DH_A
ROT_DIM = DH_A // 4
ROPE_THETA = 500000.0
H_B = 8
DH_B = 64
D_B = H_B * DH_B
LORA_W = 64
LORA_A = 64
LORA_G = 128
SHIFT_DIM = 3 * D_B + LORA_W + LORA_A + LORA_G
D_IN = 3 * D_A + SHIFT_DIM
D_FF = 2816
PAGE_SIZE = 128
NORM_EPS = 1e-6
GN_EPS = 64e-5

LANES = 128
SUBLANES = 8
VMEM_LIMIT = 48 * 1024 * 1024


def _dot(a, b):
    return jnp.dot(a, b, preferred_element_type=F32)


def _dot_nt(a, b):
    return lax.dot_general(a, b, (((1,), (1,)), ((), ())), preferred_element_type=F32)


def _rms(x, g):
    return x * lax.rsqrt(jnp.mean(x * x, axis=-1, keepdims=True) + NORM_EPS) * g


def _sigmoid(x):
    return 1.0 / (1.0 + jnp.exp(-x))


def _split2(x):
    hi = x.astype(BF16)
    mid = (x - hi.astype(F32)).astype(BF16)
    return hi, mid


def _segsum(x, ones):
    hi, mid = _split2(x)
    return _dot(hi, ones) + _dot(mid, ones)


def _params(sem):
    return pltpu.CompilerParams(dimension_semantics=sem, vmem_limit_bytes=VMEM_LIMIT)


def _ffn_half_step(x, gpre_ref, gpost_ref, wg_ref, wu_ref, wd_ref):
    un = _rms(x, gpre_ref[...]).astype(BF16)
    g = _dot(un, wg_ref[...])
    u = _dot(un, wu_ref[...])
    hid = ((g * _sigmoid(g)) * u).astype(BF16)
    return x + 0.5 * _rms(_dot(hid, wd_ref[...]), gpost_ref[...])


def _ffn_kernel(x_ref, gpre_ref, gpost_ref, wg_ref, wu_ref, wd_ref, o_ref):
    o_ref[...] = _ffn_half_step(x_ref[...], gpre_ref, gpost_ref, wg_ref, wu_ref, wd_ref)


def _ffn(x, g_pre, g_post, wg, wu, wd, tm):
    m = x.shape[0]
    row = pl.BlockSpec((tm, D_MODEL), lambda i: (i, 0))
    vec = pl.BlockSpec((1, D_MODEL), lambda i: (0, 0))
    resident = lambda w: pl.BlockSpec(w.shape, lambda i: (0, 0), pipeline_mode=pl.Buffered(1))
    return pl.pallas_call(
        _ffn_kernel,
        grid=(m // tm,),
        in_specs=[row, vec, vec, resident(wg), resident(wu), resident(wd)],
        out_specs=row,
        out_shape=jax.ShapeDtypeStruct((m, D_MODEL), F32),
        compiler_params=_params(("parallel",)),
        name="ffn",
    )(x, g_pre, g_post, wg, wu, wd)


def _proj_kernel(h_ref, g_ref, w_ref, cos_ref, sina_ref, sinb_ref,
                 qt_ref, k_ref, v_ref, kb_ref, vt_ref, pb_ref):
    u = _rms(h_ref[...], g_ref[...]).astype(BF16)
    cos = cos_ref[...]
    sina = sina_ref[...]
    sinb = sinb_ref[...]
    half = ROT_DIM // 2

    def rope(x):
        return x * cos + pltpu.roll(x, LANES - half, 1) * sina + pltpu.roll(x, half, 1) * sinb

    qa = _dot(u, w_ref[:, 0:D_A])
    ka = _dot(u, w_ref[:, D_A:2 * D_A])
    for hh in range(H_A):
        sl = slice(hh * LANES, (hh + 1) * LANES)
        qt_ref[sl, :] = (rope(qa[:, sl]) * (DH_A ** -0.5)).T.astype(BF16)
        kh = rope(ka[:, sl])
        k_ref[:, hh, :] = kh
        kb_ref[:, sl] = kh.astype(BF16)
    va = _dot(u, w_ref[:, 2 * D_A:3 * D_A])
    for hh in range(H_A):
        v_ref[:, hh, :] = va[:, hh * LANES:(hh + 1) * LANES]
    vt_ref[...] = va.T.astype(BF16)
    pb_ref[...] = _dot(u, w_ref[:, 3 * D_A:])


def _proj(h, g, w_in, tables, tm, table_blocks):
    m = h.shape[0]
    row = lambda width: pl.BlockSpec((tm, width), lambda i: (i, 0))
    tab = pl.BlockSpec((tm, LANES), lambda i: (i % table_blocks, 0))
    per_head = pl.BlockSpec((tm, H_A, 2 * DH_A), lambda i: (i, 0, 0))
    cols = pl.BlockSpec((D_A, tm), lambda i: (0, i))
    shp = lambda width, dt: jax.ShapeDtypeStruct((m, width), dt)
    return pl.pallas_call(
        _proj_kernel,
        grid=(m // tm,),
        in_specs=[row(D_MODEL), pl.BlockSpec((1, D_MODEL), lambda i: (0, 0)),
                  pl.BlockSpec((D_MODEL, D_IN), lambda i: (0, 0)), tab, tab, tab],
        out_specs=[cols, per_head, per_head, row(D_A), cols, row(SHIFT_DIM)],
        out_shape=[jax.ShapeDtypeStruct((D_A, m), BF16), jax.ShapeDtypeStruct((m, H_A, 2 * DH_A), F32),
                   jax.ShapeDtypeStruct((m, H_A, 2 * DH_A), F32), shp(D_A, BF16),
                   jax.ShapeDtypeStruct((D_A, m), BF16), shp(SHIFT_DIM, F32)],
        compiler_params=_params(("parallel",)),
        name="proj",
    )(h, g, w_in, *tables)


def _rope_tables(pos):
    half = ROT_DIM // 2
    inv_freq = ROPE_THETA ** (-jnp.arange(half, dtype=F32) / half)
    rest = jnp.zeros((DH_A - ROT_DIM,), F32)
    per_lane = jnp.concatenate([inv_freq, inv_freq, rest] * 2)
    d = jnp.arange(LANES) % DH_A
    ang = pos.astype(F32)[:, None] * per_lane[None, :]
    sin = jnp.sin(ang)
    return (jnp.cos(ang), jnp.where(d < half, -sin, 0.0),
            jnp.where((d >= half) & (d < ROT_DIM), sin, 0.0))


def _lambda(lq1_ref, lk1_ref, lq2_ref, lk2_ref, lam_init):
    s1 = jnp.sum(lq1_ref[...] * lk1_ref[...], axis=-1, keepdims=True)
    s2 = jnp.sum(lq2_ref[...] * lk2_ref[...], axis=-1, keepdims=True)
    return jnp.exp(s1) - jnp.exp(s2) + lam_init


def _attn_kernel(qt_ref, k_ref, vt_ref, lq1_ref, lk1_ref, lq2_ref, lk2_ref, subln_ref, o_ref,
                 m_ref, acc_ref, sta_ref, stb_ref, *, tq, lam_init):
    krow = lax.broadcasted_iota(jnp.int32, (tq, tq), 0)
    qcol = lax.broadcasted_iota(jnp.int32, (tq, tq), 1)
    dim = lax.broadcasted_iota(jnp.int32, (LANES, tq), 0)
    ones_rows = jnp.ones((acc_ref.shape[1] - LANES, tq), BF16)
    lam = _lambda(lq1_ref, lk1_ref, lq2_ref, lk2_ref, lam_init)
    maps = range(2)
    start = lambda i: pl.multiple_of(i * tq, tq)

    def score(qs, kstart, st_ref):
        k = k_ref[pl.ds(kstart, tq), :]
        for j in maps:
            st_ref[j] = _dot(k, qs[j])

    def consume(kstart, st_ref, diagonal):
        vt = jnp.concatenate([vt_ref[:, pl.ds(kstart, tq)], ones_rows], axis=0)
        st = [st_ref[j] for j in maps]
        if diagonal:
            st = [jnp.where(krow <= qcol, s, -jnp.inf) for s in st]
        m_prev = [m_ref[j] for j in maps]
        m_new = [jnp.maximum(m_prev[j], jnp.max(st[j], axis=0, keepdims=True)) for j in maps]
        p = [jnp.exp(st[j] - m_new[j]) for j in maps]
        alpha = [jnp.exp(m_prev[j] - m_new[j]) for j in maps]
        pv = [_dot(vt, p[j].astype(BF16)) for j in maps]
        for j in maps:
            acc_ref[j] = alpha[j] * acc_ref[j] + pv[j]
            m_ref[j] = m_new[j]

    def tile(qi, carry):
        m_ref[...] = jnp.full(m_ref.shape, -jnp.inf, F32)
        acc_ref[...] = jnp.zeros_like(acc_ref)
        qt = qt_ref[:, pl.ds(start(qi), tq)].astype(F32)
        qs = (jnp.where(dim < DH_A, qt, 0.0).astype(BF16), jnp.where(dim >= DH_A, qt, 0.0).astype(BF16))
        score(qs, 0, sta_ref)

        def body(ki, c):
            for parity, (cur, nxt) in enumerate(((sta_ref, stb_ref), (stb_ref, sta_ref))):
                @pl.when(ki % 2 == parity)
                def _(cur=cur, nxt=nxt):
                    score(qs, start(ki + 1), nxt)
                    consume(start(ki), cur, False)
            return c

        lax.fori_loop(0, qi, body, 0)
        for parity, cur in enumerate((sta_ref, stb_ref)):
            @pl.when(qi % 2 == parity)
            def _(cur=cur):
                consume(start(qi), cur, True)

        norm = lambda j: acc_ref[j, 0:LANES, :] / acc_ref[j, LANES:LANES + 1, :]
        ot = norm(0) - lam * norm(1)
        o_ref[pl.ds(start(qi), tq), :] = (_rms(ot.T, subln_ref[...]) * (1.0 - lam_init)).astype(BF16)
        return carry

    lax.fori_loop(0, qt_ref.shape[1] // tq, tile, 0)


def _attn_prompt(qt, kb, vt, lams, subln, n, t, lam_init):
    tq = 1024
    rows = pl.BlockSpec((t, LANES), lambda b, h: (b, h))
    cols = pl.BlockSpec((LANES, t), lambda b, h: (h, b))
    small = lambda w: pl.BlockSpec((1, w), lambda b, h: (0, 0))
    return pl.pallas_call(
        functools.partial(_attn_kernel, tq=tq, lam_init=lam_init),
        grid=(n, H_A),
        in_specs=[cols, rows, cols, small(DH_A), small(DH_A), small(DH_A), small(DH_A), small(2 * DH_A)],
        out_specs=rows,
        out_shape=jax.ShapeDtypeStruct((n * t, D_A), BF16),
        scratch_shapes=[pltpu.VMEM((2, 1, tq), F32), pltpu.VMEM((2, LANES + 16, tq), F32),
                        pltpu.VMEM((2, tq, tq), F32), pltpu.VMEM((2, tq, tq), F32)],
        compiler_params=_params(("parallel", "parallel")),
        name="attn_prompt",
    )(qt, kb, vt, *lams, subln)


def _attn_decode_kernel(pt_ref, q_ref, ks_ref, vs_ref, lq1_ref, lk1_ref, lq2_ref, lk2_ref, subln_ref,
                        *rest, n_pages, lam_init):
    del pt_ref
    kp_refs = rest[:n_pages]
    vp_refs = rest[n_pages:2 * n_pages]
    o_ref = rest[2 * n_pages]
    nmap = 2 * H_A
    page_rows = PAGE_SIZE * H_A
    heads = lambda x: jnp.concatenate([x[:, hh * LANES:(hh + 1) * LANES] for hh in range(H_A)], axis=0)
    q4 = heads(q_ref[0].astype(F32))
    k4 = heads(ks_ref[0].astype(BF16).astype(F32))
    v4 = heads(vs_ref[0].astype(BF16).astype(F32))
    r8 = lax.broadcasted_iota(jnp.int32, (nmap, LANES), 0)
    l8 = lax.broadcasted_iota(jnp.int32, (nmap, LANES), 1)
    q8 = jnp.where((l8 // DH_A) == (r8 // H_A), jnp.concatenate([q4, q4], axis=0), 0.0)
    q8_b = q8.astype(BF16)
    s = jnp.concatenate([_dot_nt(q8_b, kp_refs[pg][...].astype(BF16)) for pg in range(n_pages)], axis=1)
    rs = lax.broadcasted_iota(jnp.int32, s.shape, 0)
    cs = lax.broadcasted_iota(jnp.int32, s.shape, 1)
    s = jnp.where((cs % H_A) == (rs % H_A), s, -jnp.inf)
    s_self = jnp.sum(q8 * jnp.concatenate([k4, k4], axis=0), axis=1, keepdims=True)
    m = jnp.maximum(jnp.max(s, axis=1, keepdims=True), s_self)
    e = jnp.exp(s - m)
    e_self = jnp.exp(s_self - m)
    inv = 1.0 / (jnp.sum(e, axis=1, keepdims=True) + e_self)
    lam = _lambda(lq1_ref, lk1_ref, lq2_ref, lk2_ref, lam_init)
    p = e * inv
    p_self = e_self * inv
    pc = (p[0:H_A] - lam * p[H_A:nmap]).astype(BF16)
    pc_self = (p_self[0:H_A] - lam * p_self[H_A:nmap]).astype(BF16).astype(F32)
    o = pc_self * v4
    for pg in range(n_pages):
        o = o + _dot(pc[:, pg * page_rows:(pg + 1) * page_rows], vp_refs[pg][...].astype(BF16))
    o = (_rms(o, subln_ref[...]) * (1.0 - lam_init)).astype(BF16)
    for hh in range(H_A):
        o_ref[0, :, hh * LANES:(hh + 1) * LANES] = o[hh:hh + 1, :]


def _attn_decode(q, k_self, v_self, cache_k, cache_v, layer, page_table, lams, subln, lam_init):
    nb, n_pages = page_table.shape
    tok = pl.BlockSpec((1, 1, D_A), lambda b, pt: (b, 0, 0))
    small = lambda w: pl.BlockSpec((1, w), lambda b, pt: (0, 0))
    as_rows = lambda c: c.reshape(c.shape[0], c.shape[1], PAGE_SIZE * H_A, 2 * DH_A)
    cache_k, cache_v = as_rows(cache_k), as_rows(cache_v)
    page = lambda p: pl.BlockSpec((None, None, PAGE_SIZE * H_A, 2 * DH_A),
                                  lambda b, pt: (layer, pt[b, p], 0, 0))
    pages = [page(p) for p in range(n_pages)]
    grid_spec = pltpu.PrefetchScalarGridSpec(
        num_scalar_prefetch=1,
        grid=(nb,),
        in_specs=[tok, tok, tok, small(DH_A), small(DH_A), small(DH_A), small(DH_A), small(2 * DH_A)]
        + pages + pages,
        out_specs=tok,
    )
    out = pl.pallas_call(
        functools.partial(_attn_decode_kernel, n_pages=n_pages, lam_init=lam_init),
        grid_spec=grid_spec,
        out_shape=jax.ShapeDtypeStruct((nb, 1, D_A), BF16),
        compiler_params=_params(("arbitrary",)),
        name="attn_decode",
    )(page_table, q.reshape(nb, 1, D_A), k_self.reshape(nb, 1, D_A), v_self.reshape(nb, 1, D_A),
      *lams, subln, *([cache_k] * n_pages), *([cache_v] * n_pages))
    return out.reshape(nb, D_A)


def _prep_kernel(*refs, tm, tiles_per_seq):
    if tiles_per_seq:
        pb_ref, prev_ref, tail_ref = refs[:3]
        refs = refs[3:]
    else:
        pb_ref, prev_ref = refs[:2]
        refs = refs[2:]
    (mu_ref, w0_ref, a0_ref, w2_ref, a2_ref, g2_ref, kk_w_ref, ka_w_ref, rk_w_ref, ones_ref,
     kk_o, w_o, b_o, k_o, v_o, c_o, kr_o, bv_o, g_o) = refs
    pb = pb_ref[...]
    if tiles_per_seq:
        first = (pl.program_id(0) % tiles_per_seq) == 0
        prev_row = jnp.where(first, prev_ref[0], tail_ref[SUBLANES - 1:SUBLANES, :])
        rows = lax.broadcasted_iota(jnp.int32, pb.shape, 0)
        shifted = jnp.where(rows == 0, jnp.broadcast_to(prev_row, pb.shape), pltpu.roll(pb, 1, 0))
    else:
        shifted = prev_ref[...]
    xs = pb + (shifted - pb) * mu_ref[...]
    r = xs[:, 0:D_B]
    k = xs[:, D_B:2 * D_B]
    v = xs[:, 2 * D_B:3 * D_B]
    xwa = xs[:, 3 * D_B:3 * D_B + LORA_W + LORA_A]
    xg = xs[:, 3 * D_B + LORA_W + LORA_A:]
    ones = ones_ref[...]
    w_raw = w0_ref[...] + _dot(jnp.tanh(xwa).astype(BF16), w2_ref[...])
    z = -w_raw
    softplus = jnp.maximum(z, 0.0) + jnp.log(1.0 + jnp.exp(-jnp.abs(z)))
    log_decay = -jnp.exp(-softplus - 0.5)
    decay = jnp.exp(log_decay)
    a =_sigmoid(a0_ref[...] + _dot(xwa.astype(BF16), a2_ref[...]))
    g = _dot(_sigmoid(xg).astype(BF16), g2_ref[...])
    kk = k * kk_w_ref[...]
    kk = kk / jnp.maximum(jnp.sqrt(_segsum(kk * kk, ones)), 1e-12)
    k2 = k * (1.0 + (a - 1.0) * ka_w_ref[...])
    b = kk * a
    br = _segsum(b * r, ones)
    kr = _segsum(k2 * r, ones)
    bonus = _segsum(r * k2 * rk_w_ref[...], ones)
    kk_o[...] = kk
    w_o[...] = log_decay if tiles_per_seq else decay
    b_o[...] = b
    k_o[...] = k2
    v_o[...] = v
    c_o[...] = decay * r - kk * br
    kr_o[...] = kr
    bv_o[...] = bonus * v
    g_o[...] = g


def _rwkv_prep(pb, prev, weights, tm, seq_len):
    m = pb.shape[0]
    row = lambda w: pl.BlockSpec((tm, w), lambda i: (i, 0))
    const = lambda a: pl.BlockSpec(a.shape, lambda i: (0,) * a.ndim)
    if seq_len > 1:
        tiles_per_seq = seq_len // tm
        tail = pl.BlockSpec((SUBLANES, SHIFT_DIM),
                            lambda i: (jnp.maximum(i * (tm // SUBLANES) - 1, 0), 0))
        head = [row(SHIFT_DIM), pl.BlockSpec((1, 1, SHIFT_DIM), lambda i: (i // tiles_per_seq, 0, 0)), tail]
        args = [pb, prev, pb]
    else:
        tiles_per_seq = 0
        head = [row(SHIFT_DIM), row(SHIFT_DIM)]
        args = [pb, prev]
    return pl.pallas_call(
        functools.partial(_prep_kernel, tm=tm, tiles_per_seq=tiles_per_seq),
        grid=(m // tm,),
        in_specs=head + [const(a) for a in weights],
        out_specs=[row(D_B)] * 9,
        out_shape=[jax.ShapeDtypeStruct((m, D_B), F32)] * 9,
        compiler_params=_params(("parallel",)),
        name="rwkv_prep",
    )(*args, *weights)


PAIRS = H_B // 2
CHUNK = 64
GROUP = 4 * CHUNK


def _dot3(a_hi, a_mid, b_hi, b_mid):
    return _dot(a_hi, b_hi) + _dot(a_hi, b_mid) + _dot(a_mid, b_hi)


def _scan_kernel(lw_ref, kk_ref, b_ref, k_ref, v_ref, c_ref, kr_ref, tri_ref, blk_ref,
                 y_ref, sout_ref, h_ref):
    grp = pl.program_id(1)

    @pl.when(grp == 0)
    def _():
        h_ref[...] = jnp.zeros_like(h_ref)

    n = GROUP
    tri = tri_ref[...]
    blk = blk_ref[...]
    lw = lw_ref[...]
    lw_hi = lw.astype(BF16)
    lw_r = lw - lw_hi.astype(F32)
    lw_mid = lw_r.astype(BF16)
    lw_lo = (lw_r - lw_mid.astype(F32)).astype(BF16)
    cum = _dot(tri, lw_hi) + _dot(tri, lw_mid) + _dot(tri, lw_lo)
    tot = _dot(blk, lw_hi) + _dot(blk, lw_mid) + _dot(blk, lw_lo)
    g_prev = jnp.exp(cum - lw)
    g_inv = jnp.exp(-cum)
    g_end = jnp.exp(tot - cum)
    g_tot = jnp.exp(tot)
    kk = kk_ref[...]
    b = b_ref[...]
    k = k_ref[...]
    v = v_ref[...]
    at = -(kk * g_prev)
    ct = c_ref[...] * g_prev
    bt = b * g_inv
    kt = k * g_inv
    bh = b * g_end
    kh = k * g_end
    krv = kr_ref[...] * v

    ri = lax.broadcasted_iota(jnp.int32, (n, n), 0)
    ci = lax.broadcasted_iota(jnp.int32, (n, n), 1)
    first = (ri // CHUNK) * CHUNK
    stril = ((ci - first) | (ri - 1 - ci)) >= 0
    eye_n = (ri == ci).astype(F32)
    lane = lax.broadcasted_iota(jnp.int32, (n, LANES), 1)
    rown = lax.broadcasted_iota(jnp.int32, (n, LANES), 0)
    head0 = lane < DH_B
    r2 = lax.broadcasted_iota(jnp.int32, (LANES, LANES), 0)
    c2 = lax.broadcasted_iota(jnp.int32, (LANES, LANES), 1)
    same_head = (r2 // DH_B) == (c2 // DH_B)
    eye_l = r2 == c2

    heads = [(p, j) for p in range(PAIRS) for j in range(2)]
    sls = [slice(p * LANES, (p + 1) * LANES) for p in range(PAIRS)]
    bf = lambda x: x.astype(BF16)
    v_b = [bf(v[:, sl]) for sl in sls]
    bk = [bf(jnp.concatenate([bt[:, sl].T, kt[:, sl].T], axis=1)) for sl in sls]
    xs = []
    for p, j in heads:
        mine = head0 if j == 0 else lane >= DH_B
        lhs = jnp.concatenate([jnp.where(mine, at[:, sls[p]], 0.0), jnp.where(mine, ct[:, sls[p]], 0.0)], axis=0)
        xs.append(_dot(bf(lhs), bk[p]))
    lab = [jnp.where(stril, x[:n, :n], 0.0) for x in xs]
    lak = [jnp.where(stril, x[:n, n:], 0.0) for x in xs]
    mcb = [jnp.where(stril, x[n:, :n], 0.0) for x in xs]
    mck = [jnp.where(stril, x[n:, n:], 0.0) for x in xs]
    tm = [eye_n + x for x in lab]
    xb = [bf(x) for x in lab]
    xb = [bf(_dot(x, x)) for x in xb]
    for _ in range(4):
        both = [_dot(jnp.concatenate([bf(t), x], axis=0), x) for t, x in zip(tm, xb)]
        tm = [t + r[:n] for t, r in zip(tm, both)]
        xb = [bf(r[n:]) for r in both]
    tm = [t + _dot(bf(t), x) for t, x in zip(tm, xb)]
    gm = [_dot(bf(lak[i]), v_b[p]) for i, (p, j) in enumerate(heads)]
    tag = [_dot(bf(tm[i]), bf(jnp.concatenate([at[:, sls[p]], gm[i]], axis=1)))
           for i, (p, j) in enumerate(heads)]
    mt = [_dot(bf(mcb[i]), bf(tag[i])) for i in range(len(heads))]
    mv = [_dot(bf(mck[i]), v_b[p]) for i, (p, j) in enumerate(heads)]

    ta, tg, cy, yg, bht, kht, hs = [], [], [], [], [], [], []
    for p in range(PAIRS):
        pick = lambda f: jnp.where(head0, f(2 * p), f(2 * p + 1))
        ta.append(pick(lambda i: tag[i][:, :LANES]))
        tg.append(pick(lambda i: tag[i][:, LANES:]))
        cy.append(bf(ct[:, sls[p]] + pick(lambda i: mt[i][:, :LANES])))
        yg.append(pick(lambda i: mt[i][:, LANES:] + mv[i]) + krv[:, sls[p]])
        bht.append(bf(bh[:, sls[p]].T))
        kht.append(bf(kh[:, sls[p]].T))
        hs.append(h_ref[p])
    ys = [[] for _ in range(PAIRS)]
    for cidx in range(GROUP // CHUNK):
        rows = slice(cidx * CHUNK, (cidx + 1) * CHUNK)
        in_chunk = (rown // CHUNK) == cidx
        for p in range(PAIRS):
            only = lambda x: bf(jnp.where(in_chunk, x, 0.0))
            decay_c = jnp.broadcast_to(g_tot[cidx * CHUNK:cidx * CHUNK + 1, sls[p]], (LANES, LANES))
            pm = jnp.where(eye_l, decay_c, 0.0) + jnp.where(same_head, _dot(bht[p], only(ta[p])), 0.0)
            qm = jnp.where(same_head, _dot(bht[p], only(tg[p])) + _dot(kht[p], only(v[:, sls[p]])), 0.0)
            ys[p].append(_dot(cy[p][rows], bf(hs[p])) + yg[p][rows])
            hs[p] = _dot3(*_split2(pm), *_split2(hs[p])) + qm
    for p in range(PAIRS):
        h_ref[p] = hs[p]
        y_ref[:, sls[p]] = jnp.concatenate(ys[p], axis=0)

    @pl.when(grp == pl.num_programs(1) - 1)
    def _():
        for p in range(PAIRS):
            st = h_ref[p].T
            sout_ref[0, 2 * p] = st[:DH_B, :DH_B]
            sout_ref[0, 2 * p + 1] = st[DH_B:, DH_B:]


def _rwkv_scan(vecs, n, t):
    idx = jnp.arange(GROUP)
    same = (idx[:, None] // CHUNK) == (idx[None, :] // CHUNK)
    tri = (same & (idx[None, :] <= idx[:, None])).astype(BF16)
    blk_ones = same.astype(BF16)
    groups = t // GROUP
    blk = pl.BlockSpec((GROUP, D_B), lambda s, g: (s * groups + g, 0))
    const = pl.BlockSpec((GROUP, GROUP), lambda s, g: (0, 0))
    return pl.pallas_call(
        _scan_kernel,
        grid=(n, groups),
        in_specs=[blk] * 7 + [const, const],
        out_specs=[blk, pl.BlockSpec((1, H_B, DH_B, DH_B), lambda s, g: (s, 0, 0, 0))],
        out_shape=[jax.ShapeDtypeStruct((n * t, D_B), F32),
                   jax.ShapeDtypeStruct((n, H_B, DH_B, DH_B), F32)],
        scratch_shapes=[pltpu.VMEM((PAIRS, LANES, LANES), F32)],
        compiler_params=_params(("parallel", "arbitrary")),
        name="rwkv_scan",
    )(*vecs, tri, blk_ones)


def _wkv_step_kernel(s_ref, kk_ref, w_ref, b_ref, k_ref, v_ref, c_ref, kr_ref, y_ref, so_ref, t_ref):
    hd = pl.program_id(0)

    @pl.when(hd == 0)
    def _():
        for i, ref in enumerate((kk_ref, w_ref, b_ref, k_ref, v_ref, c_ref, kr_ref)):
            t_ref[i] = ref[...].T

    base = pl.multiple_of(hd * DH_B, DH_B)
    kk, w, b, k, _, c, kr = (t_ref[i, pl.ds(base, DH_B), :] for i in range(7))
    kr_row = kr[0:1, :]

    def group(gi, carry):
        v0 = pl.multiple_of(gi * SUBLANES, SUBLANES)
        v8 = t_ref[4, pl.ds(base + v0, SUBLANES), :]
        ys = []
        for j in range(SUBLANES):
            s = s_ref[v0 + j]
            v_row = v8[j:j + 1, :]
            sa = -jnp.sum(s * kk, axis=0, keepdims=True)
            ys.append(jnp.sum(s * c, axis=0, keepdims=True) + v_row * kr_row)
            so_ref[v0 + j] = s * w + sa * b + v_row * k
        y_ref[pl.ds(v0, SUBLANES), :] = jnp.concatenate(ys, axis=0)
        return carry

    lax.fori_loop(0, DH_B // SUBLANES, group, 0)


def _wkv_step(state, vecs):
    nb = state.shape[0]
    sblk = pl.BlockSpec((None, DH_B, DH_B, nb), lambda h: (h, 0, 0, 0))
    vblk = pl.BlockSpec((nb, D_B), lambda h: (0, 0))
    y_t, s_t = pl.pallas_call(
        _wkv_step_kernel,
        grid=(H_B,),
        in_specs=[sblk] + [vblk] * 7,
        out_specs=[pl.BlockSpec((DH_B, nb), lambda h: (h, 0)), sblk],
        out_shape=[jax.ShapeDtypeStruct((D_B, nb), F32),
                   jax.ShapeDtypeStruct((H_B, DH_B, DH_B, nb), F32)],
        scratch_shapes=[pltpu.VMEM((7, D_B, nb), F32)],
        compiler_params=_params(("arbitrary",)),
        name="wkv_step",
    )(jnp.transpose(state, (1, 2, 3, 0)), *vecs)
    return y_t.T, jnp.transpose(s_t, (3, 0, 1, 2))


def _mix_ffn_kernel(ya_ref, y_ref, bv_ref, g_ref, lnw_ref, lnb_ref, ones_ref, h_ref, wo_ref, gmix_ref,
                    gpre_ref, gpost_ref, wg_ref, wu_ref, wd_ref, o_ref):
    ones = ones_ref[...]
    y = y_ref[...]
    mean = _segsum(y, ones) * (1.0 / DH_B)
    d = y - mean
    var = _segsum(d * d, ones) * (1.0 / DH_B)
    yn = d * lax.rsqrt(var + GN_EPS) * lnw_ref[...] + lnb_ref[...]
    yb = ((yn + bv_ref[...]) * g_ref[...]).astype(BF16)
    mix = _dot(ya_ref[...], wo_ref[0:D_A, :]) + _dot(yb, wo_ref[D_A:, :])
    h2 = h_ref[...] + _rms(mix, gmix_ref[...])
    o_ref[...] = _ffn_half_step(h2, gpre_ref, gpost_ref, wg_ref, wu_ref, wd_ref)


def _mix_ffn(ya, y, bv, g, ln_w, ln_b, ones, h, w_out, g_mix, g_pre, g_post, wg, wu, wd, tm):
    m = h.shape[0]
    row = lambda w: pl.BlockSpec((tm, w), lambda i: (i, 0))
    resident = lambda a: pl.BlockSpec(a.shape, lambda i: (0,) * a.ndim, pipeline_mode=pl.Buffered(1))
    consts = (ln_w, ln_b, ones)
    weights = (w_out, g_mix, g_pre, g_post, wg, wu, wd)
    return pl.pallas_call(
        _mix_ffn_kernel,
        grid=(m // tm,),
        in_specs=[row(D_A), row(D_B), row(D_B), row(D_B)] + [resident(a) for a in consts]
        + [row(D_MODEL)] + [resident(a) for a in weights],
        out_specs=row(D_MODEL),
        out_shape=jax.ShapeDtypeStruct((m, D_MODEL), F32),
        compiler_params=_params(("parallel",)),
        name="mix_ffn",
    )(ya, y, bv, g, *consts, h, *weights)


def _block_ones(n, seg):
    i = jnp.arange(n) // seg
    return (i[:, None] == i[None, :]).astype(BF16)


def kernel(x_prompt, x_sample, cache_k, cache_v, state_wkv, state_shift, page_table, n_ffn1_pre, n_ffn1_post, ffn1_gate, ffn1_up, ffn1_down, n_mix_pre, n_mix_post, w_in, w_out, lambda_q1, lambda_k1, lambda_q2, lambda_k2, subln, mu_shift, w0, w2, a0, a2, g2, k_k, k_a, r_k, ln_x_w, ln_x_b, n_ffn2_pre, n_ffn2_post, ffn2_gate, ffn2_up, ffn2_down):
    n_p, t_p, _ = x_prompt.shape
    n_s, t_s, _ = x_sample.shape
    assert t_s == 1
    depth = w_in.shape[0]
    n_pages = page_table.shape[1]
    past_len = n_pages * PAGE_SIZE
    ones_seg = _block_ones(D_B, DH_B)
    tab_p = _rope_tables(jnp.arange(t_p, dtype=jnp.int32))
    tab_s = _rope_tables(jnp.full((n_s,), past_len, jnp.int32))
    zeros_lora = jnp.zeros((LORA_W, D_B), F32)
    tm_p = 512
    tm_proj = 512

    yp = x_prompt.reshape(n_p * t_p, D_MODEL)
    ys = x_sample.reshape(n_s, D_MODEL)
    outs = [[] for _ in range(8)]
    for l in range(depth):
        lam_init = 0.8 - 0.6 * math.exp(-0.3 * l)
        vec = lambda a: a[l].reshape(1, -1)
        ffn1 = (vec(n_ffn1_pre), vec(n_ffn1_post), ffn1_gate[l].astype(BF16), ffn1_up[l].astype(BF16),
                ffn1_down[l].astype(BF16))
        ffn2 = (vec(n_ffn2_pre), vec(n_ffn2_post), ffn2_gate[l].astype(BF16), ffn2_up[l].astype(BF16),
                ffn2_down[l].astype(BF16))
        w_in_b = w_in[l].astype(BF16)
        w_out_b = w_out[l].astype(BF16)
        lams = (vec(lambda_q1), vec(lambda_k1), vec(lambda_q2), vec(lambda_k2))
        prep_w = (vec(mu_shift), vec(w0), vec(a0),
                  jnp.concatenate([w2[l], zeros_lora], axis=0).astype(BF16),
                  jnp.concatenate([zeros_lora, a2[l]], axis=0).astype(BF16),
                  g2[l].astype(BF16), vec(k_k), vec(k_a), r_k[l].reshape(1, D_B), ones_seg)

        def mix_tail(h, ya, y, bv, g, tm):
            return _mix_ffn(ya, y, bv, g, vec(ln_x_w), vec(ln_x_b), ones_seg, h, w_out_b, vec(n_mix_post),
                            *ffn2, tm)

        h = _ffn(yp, *ffn1, tm_p)
        qt, k, v, kb, vt, pb = _proj(h, vec(n_mix_pre), w_in_b, tab_p, tm_proj, t_p // tm_proj)
        ya = _attn_prompt(qt, kb, vt, lams, vec(subln), n_p, t_p, lam_init)
        prev0 = jnp.zeros((n_p, 1, SHIFT_DIM), F32)
        kk_, lw_, b_, k2_, v_, c_, kr_, bv_, g_ = _rwkv_prep(pb, prev0, prep_w, tm_proj, t_p)
        y, s_new = _rwkv_scan((lw_, kk_, b_, k2_, v_, c_, kr_), n_p, t_p)
        yp = mix_tail(h, ya, y, bv_, g_, tm_p)
        outs[0].append(k.reshape(n_p, t_p, H_A, 2 * DH_A))
        outs[1].append(v.reshape(n_p, t_p, H_A, 2 * DH_A))
        outs[2].append(s_new)
        outs[3].append(pb.reshape(n_p, t_p, SHIFT_DIM)[:, -1])

        h = _ffn(ys, *ffn1, n_s)
        qt, k, v, _, _, pb = _proj(h, vec(n_mix_pre), w_in_b, tab_s, n_s, 1)
        ya = _attn_decode(qt.T, k, v, cache_k, cache_v, l, page_table, lams, vec(subln), lam_init)
        kk_, w_, b_, k2_, v_, c_, kr_, bv_, g_ = _rwkv_prep(pb, state_shift[l], prep_w, n_s, 1)
        y, s_new = _wkv_step(state_wkv[l], (kk_, w_, b_, k2_, v_, c_, kr_))
        ys = mix_tail(h, ya, y, bv_, g_, n_s)
        outs[4].append(k.reshape(n_s, 1, H_A, 2 * DH_A))
        outs[5].append(v.reshape(n_s, 1, H_A, 2 * DH_A))
        outs[6].append(s_new)
        outs[7].append(pb)

    return (yp.reshape(n_p, t_p, D_MODEL), ys.reshape(n_s, 1, D_MODEL),
            *[jnp.stack(o) for o in outs])
```
